```python
import math
import jax, jax.numpy as jnp
from jax import lax
import numpy as np

D_MODEL = 4096
BATCH = 4
SEQ = 2048
DEPTH = 4
DEC_BATCH = 8
DEC_SEQ = 4
PAST_LEN = 8192
PAGE_SIZE = 128

N_MIXERS = 4
D_FF = 2 * D_MODEL
DN_ALPHA = (2.0 * DEPTH) ** 0.25
DN_BETA = (8.0 * DEPTH) ** -0.25
LN_EPS = 1e-5
ROPE_THETA = 10000.0
NEG_INF = -1e30

SSM_GROUP = 16
SSM_GROUPS = D_MODEL // SSM_GROUP
SSM_N = 64

RET_DK = 256
RET_HEADS = D_MODEL // RET_DK
RET_DV = 2 * RET_DK
RET_CHUNK = 128

NSA_DH = 128
NSA_HEADS = D_MODEL // NSA_DH
NSA_KVH = 4
NSA_REP = NSA_HEADS // NSA_KVH
CMP_BLOCK = 32
SEL_BLOCK = 64
N_SEL = 16
WINDOW = 512
SEL_QBLK = 64
WIN_QBLK = 128
SEL_FORCE = 1e4

POOL_WINDOWS = (2, 4, 8, 16)
POOL_CH = D_MODEL // len(POOL_WINDOWS)
POOL_PAST = max(POOL_WINDOWS) - 1

kernel_name = 'hybrid_s5_retention_nsa_pool_step'


def _n_layers_of(m):
    return len(range(m, DEPTH, N_MIXERS))


def _layer_norm(x, g, b):
    xf = x.astype(jnp.float32)
    mu = jnp.mean(xf, -1, keepdims=True)
    var = jnp.mean(jnp.square(xf - mu), -1, keepdims=True)
    return (xf - mu) * lax.rsqrt(var + LN_EPS) * g + b


def _post_norm(x, f_out, g, b):
    return _layer_norm(DN_ALPHA * x + f_out, g, b)


def _swiglu(x, w_gate, w_up, w_down):
    return (jax.nn.silu(x @ w_gate) * (x @ w_up)) @ w_down


def _rope_freqs(dim):
    return 1.0 / (ROPE_THETA ** (jnp.arange(0, dim, 2, dtype=jnp.float32) / dim))


def _retnet_freqs(dim):
    return 1.0 / (ROPE_THETA ** jnp.linspace(0.0, 1.0, dim // 2, dtype=jnp.float32))


def _rope(x, pos, inv_freq):
    ang = pos.astype(jnp.float32)[:, None] * inv_freq[None, :]
    cos = jnp.cos(ang)[:, None, :]
    sin = jnp.sin(ang)[:, None, :]
    x1, x2 = jnp.split(x, 2, axis=-1)
    return jnp.concatenate([x1 * cos - x2 * sin, x2 * cos + x1 * sin], axis=-1)


def _complex_affine_combine(e1, e2):
    a1r, a1i, b1r, b1i = e1
    a2r, a2i, b2r, b2i = e2
    return (a2r * a1r - a2i * a1i, a2r * a1i + a2i * a1r,
            a2r * b1r - a2i * b1i + b2r, a2r * b1i + a2i * b1r + b2i)


def _ssm_mixer(x, h0, a_re, a_im, log_dt, b_re, b_im, c_re, c_im, d_skip, w_glu):
    f32 = jnp.float32
    bt, L, _ = x.shape
    u = x.reshape(bt, L, SSM_GROUPS, SSM_GROUP)
    ar = a_re.astype(f32)
    ai = a_im.astype(f32)
    dt = jnp.exp(log_dt.astype(f32))[:, None]
    mag = jnp.exp(ar * dt)
    abar_r = mag * jnp.cos(ai * dt)
    abar_i = mag * jnp.sin(ai * dt)
    den = ar * ar + ai * ai
    zr = abar_r - 1.0
    coef_r = (zr * ar + abar_i * ai) / den
    coef_i = (abar_i * ar - zr * ai) / den
    br = b_re.astype(f32)
    bi = b_im.astype(f32)
    bbar_r = coef_r[..., None] * br - coef_i[..., None] * bi
    bbar_i = coef_r[..., None] * bi + coef_i[..., None] * br
    bu_r = jnp.einsum('blgc,gnc->lbgn', u, bbar_r)
    bu_i = jnp.einsum('blgc,gnc->lbgn', u, bbar_i)
    a_seq_r = jnp.broadcast_to(abar_r[None, None], (L, 1, SSM_GROUPS, SSM_N))
    a_seq_i = jnp.broadcast_to(abar_i[None, None], (L, 1, SSM_GROUPS, SSM_N))
    acc_r, acc_i, h_r, h_i = lax.associative_scan(
        _complex_affine_combine, (a_seq_r, a_seq_i, bu_r, bu_i), axis=0)
    h0 = h0.astype(f32)
    h0r, h0i = h0[..., 0], h0[..., 1]
    h_r, h_i = (h_r + acc_r * h0r - acc_i * h0i, h_i + acc_r * h0i + acc_i * h0r)
    y = (jnp.einsum('lbgn,gcn->blgc', h_r, c_re) - jnp.einsum('lbgn,gcn->blgc', h_i, c_im)
         + u * d_skip)
    y = jax.nn.gelu(y.reshape(bt, L, D_MODEL))
    val, gate = jnp.split(y @ w_glu, 2, axis=-1)
    return val * jax.nn.sigmoid(gate), jnp.stack([h_r[-1], h_i[-1]], axis=-1)


def _retention_mixer(x, pos0, s0, w_in, norm_g, w_out):
    f32 = jnp.float32
    bt, L, _ = x.shape
    H = RET_HEADS
    dqk = H * RET_DK
    dv = H * RET_DV
    q, k, v, g = jnp.split(x @ w_in, [dqk, 2 * dqk, 2 * dqk + dv], axis=-1)
    pos = pos0 + jnp.arange(L)
    inv = _retnet_freqs(RET_DK)
    q = _rope(q.reshape(bt, L, H, RET_DK), pos, inv).astype(f32)
    k = (_rope(k.reshape(bt, L, H, RET_DK), pos, inv) * RET_DK ** -0.5).astype(f32)
    v = v.reshape(bt, L, H, RET_DV).astype(f32)
    C = math.gcd(L, RET_CHUNK)
    nc = L // C
    log_g = jnp.log1p(-jnp.exp2(-5.0 - jnp.arange(H, dtype=f32)))
    idx = jnp.arange(C, dtype=f32)
    diff = idx[:, None] - idx[None, :]
    intra = jnp.where(diff >= 0, jnp.exp(log_g[:, None, None] * jnp.maximum(diff, 0.0)), 0.0)
    q_dec = jnp.exp(log_g[None, :] * (idx[:, None] + 1.0))
    k_dec = jnp.exp(log_g[None, :] * (C - 1.0 - idx[:, None]))
    chunk_dec = jnp.exp(log_g * C)

    def to_chunks(t):
        return jnp.moveaxis(t.reshape(bt, nc, C, *t.shape[2:]), 1, 0)

    def step(s, inp):
        qc, kc, vc = inp
        sc = jnp.einsum('bihd,bjhd->bhij', qc, kc) * intra
        o = jnp.einsum('bhij,bjhe->bihe', sc, vc)
        o = o + jnp.einsum('bihd,bhde->bihe', qc, s) * q_dec[None, :, :, None]
        s = (s * chunk_dec[None, :, None, None]
             + jnp.einsum('bjhd,bjhe->bhde', kc * k_dec[None, :, :, None], vc))
        return s, o

    s_fin, o = lax.scan(step, s0.astype(f32), (to_chunks(q), to_chunks(k), to_chunks(v)))
    o = jnp.moveaxis(o, 0, 1).reshape(bt, L, H, RET_DV)
    mu = jnp.mean(o, -1, keepdims=True)
    var = jnp.mean(jnp.square(o - mu), -1, keepdims=True)
    o = ((o - mu) * lax.rsqrt(var + LN_EPS)).reshape(bt, L, dv) * norm_g
    return (jax.nn.silu(g) * o) @ w_out, s_fin


def _compress(blk, pe, w1, w2):
    bt, nb = blk.shape[0], blk.shape[1]
    h = jnp.moveaxis(blk + pe[None, None, :, None, :], 2, 3).reshape(bt, nb, NSA_KVH, CMP_BLOCK * NSA_DH)
    return jax.nn.gelu(h @ w1) @ w2


def _nsa_mixer(x, pos0, past_cmp, past_slc, win_prefix, n_keep,
               w_in, pe_ck, w_ck1, w_ck2, pe_cv, w_cv1, w_cv2, w_out):
    bt, L, _ = x.shape
    G, R, dh = NSA_KVH, NSA_REP, NSA_DH
    kvw = G * dh
    qw_ = NSA_HEADS * dh
    splits = [qw_ + j * kvw for j in range(7)]
    q, kc, vc, ks, vs, kw, vw, gates = jnp.split(x @ w_in, splits, axis=-1)
    pos = pos0 + jnp.arange(L)
    inv = _rope_freqs(dh)
    q = _rope(q.reshape(bt, L, NSA_HEADS, dh), pos, inv).reshape(bt, L, G, R, dh) * dh ** -0.5
    new_cmp = jnp.stack([kc.reshape(bt, L, G, dh), vc.reshape(bt, L, G, dh)], axis=2)
    new_slc = jnp.stack([_rope(ks.reshape(bt, L, G, dh), pos, inv), vs.reshape(bt, L, G, dh)], axis=2)
    new_win = jnp.stack([_rope(kw.reshape(bt, L, G, dh), pos, inv), vw.reshape(bt, L, G, dh)], axis=2)
    all_cmp = new_cmp if past_cmp is None else jnp.concatenate([past_cmp, new_cmp], axis=1)
    all_slc = new_slc if past_slc is None else jnp.concatenate([past_slc, new_slc], axis=1)
    T = all_cmp.shape[1]
    Tp = -(-T // SEL_BLOCK) * SEL_BLOCK
    pad = ((0, 0), (0, Tp - T), (0, 0), (0, 0), (0, 0))
    all_cmp = jnp.pad(all_cmp, pad)
    all_slc = jnp.pad(all_slc, pad)

    nb = Tp // CMP_BLOCK
    blk = all_cmp.reshape(bt, nb, CMP_BLOCK, 2, G, dh)
    blk_end = (jnp.arange(nb) + 1) * CMP_BLOCK - 1
    ck = _rope(_compress(blk[:, :, :, 0], pe_ck, w_ck1, w_ck2), blk_end, inv)
    cv = _compress(blk[:, :, :, 1], pe_cv, w_cv1, w_cv2)
    m_cmp = (blk_end[None, :] <= pos[:, None])[None, :, None, None, :]
    s_cmp = jnp.einsum('blgrd,bngd->blgrn', q, ck)
    p_cmp = jax.nn.softmax(jnp.where(m_cmp, s_cmp, NEG_INF), axis=-1) * m_cmp
    o_cmp = jnp.einsum('blgrn,bngd->blgrd', p_cmp, cv)

    nsel = Tp // SEL_BLOCK
    imp = p_cmp.sum(axis=3).reshape(bt, L, G, nsel, SEL_BLOCK // CMP_BLOCK).sum(-1)
    j = jnp.arange(nsel)[None, :]
    cur = (pos // SEL_BLOCK)[:, None]
    forced = ((j == 0) | (j == cur) | (j == cur - 1))[None, :, None, :]
    future = (j * SEL_BLOCK > pos[:, None])[None, :, None, :]
    score = jnp.where(forced, SEL_FORCE, jnp.where(future, -SEL_FORCE, imp))
    k_sel = min(N_SEL, nsel)
    _, sel_idx = lax.top_k(score, k_sel)

    ks_blk = jnp.moveaxis(all_slc[:, :, 0].reshape(bt, nsel, SEL_BLOCK, G, dh), 3, 1)
    vs_blk = jnp.moveaxis(all_slc[:, :, 1].reshape(bt, nsel, SEL_BLOCK, G, dh), 3, 1)
    b_ix = jnp.arange(bt)[:, None, None, None]
    g_ix = jnp.arange(G)[None, None, :, None]
    qb = math.gcd(L, SEL_QBLK)
    nq = L // qb

    def sel_block(args):
        qq, ii, pp = args
        kk = ks_blk[b_ix, g_ix, ii]
        vv = vs_blk[b_ix, g_ix, ii]
        s = jnp.einsum('bqgrd,bqgkjd->bqgrkj', qq, kk)
        kpos = ii[..., None] * SEL_BLOCK + jnp.arange(SEL_BLOCK)
        msk = (kpos <= pp[None, :, None, None, None])[:, :, :, None]
        s = jnp.where(msk, s, NEG_INF).reshape(bt, qb, G, R, k_sel * SEL_BLOCK)
        p = jax.nn.softmax(s, axis=-1).reshape(bt, qb, G, R, k_sel, SEL_BLOCK)
        return jnp.einsum('bqgrkj,bqgkjd->bqgrd', p, vv)

    o_slc = lax.map(sel_block, (jnp.moveaxis(q.reshape(bt, nq, qb, G, R, dh), 1, 0),
                                jnp.moveaxis(sel_idx.reshape(bt, nq, qb, G, k_sel), 1, 0),
                                pos.reshape(nq, qb)))
    o_slc = jnp.moveaxis(o_slc, 0, 1).reshape(bt, L, G, R, dh)

    P = win_prefix.shape[1]
    win_all = jnp.concatenate([win_prefix, new_win], axis=1)
    kw_all, vw_all = win_all[:, :, 0], win_all[:, :, 1]
    qwb = math.gcd(L, WIN_QBLK)
    nqw = L // qwb
    span = P + qwb

    def win_block(i):
        st = i * qwb
        qq = lax.dynamic_slice_in_dim(q, st, qwb, axis=1)
        kk = lax.dynamic_slice_in_dim(kw_all, st, span, axis=1)
        vv = lax.dynamic_slice_in_dim(vw_all, st, span, axis=1)
        qpos = pos0 + st + jnp.arange(qwb)
        kpos = pos0 - P + st + jnp.arange(span)
        msk = ((kpos[None, :] >= 0) & (kpos[None, :] <= qpos[:, None])
               & (kpos[None, :] > qpos[:, None] - WINDOW))[None, :, None, None, :]
        s = jnp.einsum('bqgrd,bkgd->bqgrk', qq, kk)
        p = jax.nn.softmax(jnp.where(msk, s, NEG_INF), axis=-1)
        return jnp.einsum('bqgrk,bkgd->bqgrd', p, vv)

    o_win = lax.map(win_block, jnp.arange(nqw))
    o_win = jnp.moveaxis(o_win, 0, 1).reshape(bt, L, G, R, dh)

    gt = jax.nn.sigmoid(gates).reshape(bt, L, G, R, 3)
    o = gt[..., 0:1] * o_cmp + gt[..., 1:2] * o_slc + gt[..., 2:3] * o_win
    out = o.reshape(bt, L, NSA_HEADS * dh) @ w_out
    return out, new_cmp, new_slc, win_all[:, -n_keep:]


def _pool_mixer(x, pos0, prefix, w_pool, scale):
    f32 = jnp.float32
    bt, L, _ = x.shape
    P = prefix.shape[1]
    xx = jnp.concatenate([prefix.astype(f32), x], axis=1)
    cs = jnp.concatenate([jnp.zeros((bt, 1, D_MODEL), f32), jnp.cumsum(xx, axis=1)], axis=1)
    end = cs[:, P + 1:]
    pos = pos0 + jnp.arange(L)
    outs = []
    for gi, w in enumerate(POOL_WINDOWS):
        sl = slice(gi * POOL_CH, (gi + 1) * POOL_CH)
        start = cs[:, P + 1 - w:P + 1 - w + L, sl]
        cnt = jnp.minimum(w, pos + 1).astype(f32)[None, :, None]
        mean = (end[:, :, sl] - start) / cnt
        outs.append((mean - x[:, :, sl]) @ w_pool[gi])
    return jnp.concatenate(outs, axis=-1) * scale, xx[:, -POOL_PAST:]


def setup_inputs(seed: int = 0) -> dict:
    key = jax.random.key(seed)
    keys = iter(jax.random.split(key, 64))
    f32 = jnp.float32

    def nrm(shape, scale):
        return jax.random.normal(next(keys), shape, f32) * scale

    n_a, n_b, n_c, n_d = (_n_layers_of(m) for m in range(N_MIXERS))
    n_pages = PAST_LEN // PAGE_SIZE
    n_used = DEC_BATCH * n_pages
    n_pool = n_used + max(1, n_used // 4)
    win_buf = min(WINDOW, PAST_LEN)
    kvw = NSA_KVH * NSA_DH
    nsa_in = NSA_HEADS * NSA_DH + 6 * kvw + 3 * NSA_HEADS
    ret_in = 2 * RET_HEADS * RET_DK + 2 * RET_HEADS * RET_DV
    page_table = jax.random.permutation(next(keys), n_pool)[:n_used].reshape(DEC_BATCH, n_pages).astype(jnp.int32)
    return {
        'x_prompt': nrm((BATCH, SEQ, D_MODEL), 1.0),
        'x_sample': nrm((DEC_BATCH, DEC_SEQ, D_MODEL), 1.0),
        'state_ssm': nrm((n_a, DEC_BATCH, SSM_GROUPS, SSM_N, 2), 0.1),
        'state_ret': nrm((n_b, DEC_BATCH, RET_HEADS, RET_DK, RET_DV), 0.5),
        'cache_cmp_kv': nrm((n_c, n_pool, PAGE_SIZE, 2, NSA_KVH, NSA_DH), 1.0),
        'cache_slc_kv': nrm((n_c, n_pool, PAGE_SIZE, 2, NSA_KVH, NSA_DH), 1.0),
        'cache_win_kv': nrm((n_c, DEC_BATCH, win_buf, 2, NSA_KVH, NSA_DH), 1.0),
        'state_pool': nrm((n_d, DEC_BATCH, POOL_PAST, D_MODEL), 1.0),
        'page_table': page_table,
        'ln_g': 1.0 + nrm((DEPTH, 3, D_MODEL), 0.02),
        'ln_b': nrm((DEPTH, 3, D_MODEL), 0.02),
        'ffn_w_gate': nrm((DEPTH, 2, D_MODEL, D_FF), D_MODEL ** -0.5),
        'ffn_w_up': nrm((DEPTH, 2, D_MODEL, D_FF), D_MODEL ** -0.5),
        'ffn_w_down': nrm((DEPTH, 2, D_FF, D_MODEL), DN_BETA * D_FF ** -0.5),
        'ssm_a_re': -0.5 * jnp.exp(nrm((n_a, SSM_GROUPS, SSM_N), 0.05)),
        'ssm_a_im': math.pi * jnp.arange(SSM_N, dtype=f32) + nrm((n_a, SSM_GROUPS, SSM_N), 0.05),
        'ssm_log_dt': jax.random.uniform(next(keys), (n_a, SSM_GROUPS), f32, math.log(1e-3), math.log(1e-1)),
        'ssm_b_re': nrm((n_a, SSM_GROUPS, SSM_N, SSM_GROUP), (2 * SSM_GROUP) ** -0.5),
        'ssm_b_im': nrm((n_a, SSM_GROUPS, SSM_N, SSM_GROUP), (2 * SSM_GROUP) ** -0.5),
        'ssm_c_re': nrm((n_a, SSM_GROUPS, SSM_GROUP, SSM_N), SSM_N ** -0.5),
        'ssm_c_im': nrm((n_a, SSM_GROUPS, SSM_GROUP, SSM_N), SSM_N ** -0.5),
        'ssm_d': nrm((n_a, SSM_GROUPS, SSM_GROUP), 0.5),
        'ssm_w_glu': nrm((n_a, D_MODEL, 2 * D_MODEL), DN_BETA * D_MODEL ** -0.5),
        'ret_w_in': nrm((n_b, D_MODEL, ret_in), D_MODEL ** -0.5),
        'ret_norm_g': 1.0 + nrm((n_b, RET_HEADS * RET_DV), 0.02),
        'ret_w_out': nrm((n_b, RET_HEADS * RET_DV, D_MODEL), DN_BETA * (RET_HEADS * RET_DV) ** -0.5),
        'nsa_w_in': nrm((n_c, D_MODEL, nsa_in), D_MODEL ** -0.5),
        'nsa_pe_ck': nrm((n_c, CMP_BLOCK, NSA_DH), 0.1),
        'nsa_w_ck1': nrm((n_c, CMP_BLOCK * NSA_DH, NSA_DH), (CMP_BLOCK * NSA_DH) ** -0.5),
        'nsa_w_ck2': nrm((n_c, NSA_DH, NSA_DH), NSA_DH ** -0.5),
        'nsa_pe_cv': nrm((n_c, CMP_BLOCK, NSA_DH), 0.1),
        'nsa_w_cv1': nrm((n_c, CMP_BLOCK * NSA_DH, NSA_DH), (CMP_BLOCK * NSA_DH) ** -0.5),
        'nsa_w_cv2': nrm((n_c, NSA_DH, NSA_DH), NSA_DH ** -0.5),
        'nsa_w_out': nrm((n_c, NSA_HEADS * NSA_DH, D_MODEL), DN_BETA * (NSA_HEADS * NSA_DH) ** -0.5),
        'pool_w': nrm((n_d, len(POOL_WINDOWS), POOL_CH, POOL_CH), DN_BETA * POOL_CH ** -0.5),
        'pool_scale': 1.0 + nrm((n_d, D_MODEL), 0.02),
    }


def reference(x_prompt, x_sample, state_ssm, state_ret, cache_cmp_kv, cache_slc_kv, cache_win_kv,
              state_pool, page_table, ln_g, ln_b, ffn_w_gate, ffn_w_up, ffn_w_down,
              ssm_a_re, ssm_a_im, ssm_log_dt, ssm_b_re, ssm_b_im, ssm_c_re, ssm_c_im, ssm_d, ssm_w_glu,
              ret_w_in, ret_norm_g, ret_w_out,
              nsa_w_in, nsa_pe_ck, nsa_w_ck1, nsa_w_ck2, nsa_pe_cv, nsa_w_cv1, nsa_w_cv2, nsa_w_out,
              pool_w, pool_scale):
    f32 = jnp.float32
    xp = x_prompt.astype(f32)
    xs = x_sample.astype(f32)
    bp, bs = xp.shape[0], xs.shape[0]
    past_len = page_table.shape[1] * PAGE_SIZE
    ssm_p, ssm_s, ret_p, ret_s = [], [], [], []
    cmp_p, cmp_s, slc_p, slc_s, win_p, win_s = [], [], [], [], [], []
    pool_p, pool_s = [], []
    for i in range(DEPTH):
        m, li = i % N_MIXERS, i // N_MIXERS
        ffn0 = (ffn_w_gate[i, 0], ffn_w_up[i, 0], ffn_w_down[i, 0])
        ffn1 = (ffn_w_gate[i, 1], ffn_w_up[i, 1], ffn_w_down[i, 1])
        xp = _post_norm(xp, 0.5 * _swiglu(xp, *ffn0), ln_g[i, 0], ln_b[i, 0])
        xs = _post_norm(xs, 0.5 * _swiglu(xs, *ffn0), ln_g[i, 0], ln_b[i, 0])
        if m == 0:
            prm = (ssm_a_re[li], ssm_a_im[li], ssm_log_dt[li], ssm_b_re[li], ssm_b_im[li],
                   ssm_c_re[li], ssm_c_im[li], ssm_d[li], ssm_w_glu[li])
            yp, st_p = _ssm_mixer(xp, jnp.zeros((bp, SSM_GROUPS, SSM_N, 2), f32), *prm)
            ys, st_s = _ssm_mixer(xs, state_ssm[li], *prm)
            ssm_p.append(st_p)
            ssm_s.append(st_s)
        elif m == 1:
            prm = (ret_w_in[li], ret_norm_g[li], ret_w_out[li])
            yp, st_p = _retention_mixer(xp, 0, jnp.zeros((bp, RET_HEADS, RET_DK, RET_DV), f32), *prm)
            ys, st_s = _retention_mixer(xs, past_len, state_ret[li], *prm)
            ret_p.append(st_p)
            ret_s.append(st_s)
        elif m == 2:
            prm = (nsa_w_in[li], nsa_pe_ck[li], nsa_w_ck1[li], nsa_w_ck2[li],
                   nsa_pe_cv[li], nsa_w_cv1[li], nsa_w_cv2[li], nsa_w_out[li])
            past_c = cache_cmp_kv[li][page_table].reshape(bs, past_len, 2, NSA_KVH, NSA_DH)
            past_s = cache_slc_kv[li][page_table].reshape(bs, past_len, 2, NSA_KVH, NSA_DH)
            yp, c_new, s_new, w_new = _nsa_mixer(
                xp, 0, None, None, jnp.zeros((bp, WINDOW, 2, NSA_KVH, NSA_DH), f32),
                min(WINDOW, xp.shape[1]), *prm)
            cmp_p.append(c_new)
            slc_p.append(s_new)
            win_p.append(w_new)
            ys, c_new, s_new, w_new = _nsa_mixer(
                xs, past_len, past_c, past_s, cache_win_kv[li], cache_win_kv.shape[2], *prm)
            cmp_s.append(c_new)
            slc_s.append(s_new)
            win_s.append(w_new)
        else:
            yp, st_p = _pool_mixer(xp, 0, jnp.zeros((bp, POOL_PAST, D_MODEL), f32), pool_w[li], pool_scale[li])
            ys, st_s = _pool_mixer(xs, past_len, state_pool[li], pool_w[li], pool_scale[li])
            pool_p.append(st_p)
            pool_s.append(st_s)
        xp = _post_norm(xp, yp, ln_g[i, 1], ln_b[i, 1])
        xs = _post_norm(xs, ys, ln_g[i, 1], ln_b[i, 1])
        xp = _post_norm(xp, 0.5 * _swiglu(xp, *ffn1), ln_g[i, 2], ln_b[i, 2])
        xs = _post_norm(xs, 0.5 * _swiglu(xs, *ffn1), ln_g[i, 2], ln_b[i, 2])
    return (xp.astype(x_prompt.dtype), xs.astype(x_sample.dtype),
            jnp.stack(ssm_p), jnp.stack(ssm_s), jnp.stack(ret_p), jnp.stack(ret_s),
            jnp.stack(cmp_p), jnp.stack(cmp_s), jnp.stack(slc_p), jnp.stack(slc_s),
            jnp.stack(win_p), jnp.stack(win_s), jnp.stack(pool_p), jnp.stack(pool_s))
```

```python
import functools
import math

import jax
import jax.numpy as jnp
from jax import lax
from jax.experimental import pallas as pl
from jax.experimental.pallas import tpu as pltpu

F32 = jnp.float32
BF16 = jnp.bfloat16

D_MODEL = 4096
DEPTH = 4
PAGE_SIZE = 128
N_MIXERS = 4
D_FF = 2 * D_MODEL
DN_ALPHA = (2.0 * DEPTH) ** 0.25
LN_EPS = 1e-5
ROPE_THETA = 10000.0
NEG_INF = -1e30

SSM_GROUP = 16
SSM_GROUPS = D_MODEL // SSM_GROUP
SSM_N = 64

RET_DK = 256
RET_HEADS = D_MODEL // RET_DK
RET_DV = 2 * RET_DK
RET_CHUNK = 128

NSA_DH = 128
NSA_HEADS = D_MODEL // NSA_DH
NSA_KVH = 4
NSA_REP = NSA_HEADS // NSA_KVH
CMP_BLOCK = 32
SEL_BLOCK = 64
N_SEL = 16
WINDOW = 512
SEL_QBLK = 64
WIN_QBLK = 128
SEL_FORCE = 1e4

POOL_WINDOWS = (2, 4, 8, 16)
POOL_CH = D_MODEL // len(POOL_WINDOWS)
POOL_PAST = max(POOL_WINDOWS) - 1

V7X_VMEM_BYTES = 64 * 1024 * 1024
VMEM_LIMIT = V7X_VMEM_BYTES - 8 * 1024 * 1024


def _cparams(*sem):
    return pltpu.CompilerParams(dimension_semantics=sem, vmem_limit_bytes=VMEM_LIMIT)


def _mm_kernel(x_ref, w_ref, o_ref):
    o_ref[...] = jnp.dot(x_ref[...], w_ref[...], preferred_element_type=F32).astype(o_ref.dtype)


def _w_spec(w, k, bn, col0_blocks, widx):
    lead = tuple(widx)
    shape = (None,) * len(lead) + (k, bn)
    return pl.BlockSpec(shape, lambda i, j: lead + (0, j + col0_blocks))


def mm(x, w, *, widx=(), n=None, col0=0, out_dtype=F32, bm=None, bn=512):
    m, k = x.shape
    n = w.shape[-1] if n is None else n
    if bm is None:
        bm = min(m, 1024 if k <= 4096 else 512)
    bn = min(bn, n)
    assert m % bm == 0 and n % bn == 0 and col0 % bn == 0
    return pl.pallas_call(
        _mm_kernel,
        grid=(m // bm, n // bn),
        in_specs=[pl.BlockSpec((bm, k), lambda i, j: (i, 0)),
                  _w_spec(w, k, bn, col0 // bn, widx)],
        out_specs=pl.BlockSpec((bm, bn), lambda i, j: (i, j)),
        out_shape=jax.ShapeDtypeStruct((m, n), out_dtype),
        compiler_params=_cparams("parallel", "arbitrary"),
        name="mm",
    )(x, w)


def _dual_mm_kernel(x_ref, wa_ref, wb_ref, o_ref, *, combine):
    x = x_ref[...]
    a = jnp.dot(x, wa_ref[...], preferred_element_type=F32)
    b = jnp.dot(x, wb_ref[...], preferred_element_type=F32)
    o_ref[...] = combine(a, b).astype(o_ref.dtype)


def _swiglu_combine(a, b):
    return a * jax.nn.sigmoid(a) * b


def _glu_combine(a, b):
    return a * jax.nn.sigmoid(b)


def dual_mm(x, wa, wb, *, combine, n, widx_a=(), widx_b=(), col0_a=0, col0_b=0,
            out_dtype=BF16, bm=None, bn=512):
    m, k = x.shape
    if bm is None:
        bm = min(m, 1024)
    assert m % bm == 0 and n % bn == 0 and col0_a % bn == 0 and col0_b % bn == 0
    return pl.pallas_call(
        functools.partial(_dual_mm_kernel, combine=combine),
        grid=(m // bm, n // bn),
        in_specs=[pl.BlockSpec((bm, k), lambda i, j: (i, 0)),
                  _w_spec(wa, k, bn, col0_a // bn, widx_a),
                  _w_spec(wb, k, bn, col0_b // bn, widx_b)],
        out_specs=pl.BlockSpec((bm, bn), lambda i, j: (i, j)),
        out_shape=jax.ShapeDtypeStruct((m, n), out_dtype),
        compiler_params=_cparams("parallel", "arbitrary"),
        name="dual_mm",
    )(x, wa, wb)


def _res_ln_kernel(y_ref, r_ref, g_ref, b_ref, of_ref, ob_ref, *, scale):
    z = DN_ALPHA * r_ref[...] + scale * y_ref[...].astype(F32)
    mu = jnp.mean(z, axis=-1, keepdims=True)
    zc = z - mu
    var = jnp.mean(zc * zc, axis=-1, keepdims=True)
    out = zc * lax.rsqrt(var + LN_EPS) * g_ref[...] + b_ref[...]
    of_ref[...] = out
    ob_ref[...] = out.astype(BF16)


def res_ln(y, resid, g, b, *, scale=1.0, bm=256):
    m, d = resid.shape
    bm = min(bm, m)
    assert m % bm == 0
    row = pl.BlockSpec((bm, d), lambda i: (i, 0))
    vec = pl.BlockSpec((1, d), lambda i: (0, 0))
    return pl.pallas_call(
        functools.partial(_res_ln_kernel, scale=scale),
        grid=(m // bm,),
        in_specs=[row, row, vec, vec],
        out_specs=[row, row],
        out_shape=[jax.ShapeDtypeStruct((m, d), F32), jax.ShapeDtypeStruct((m, d), BF16)],
        compiler_params=_cparams("parallel"),
        name="res_ln",
    )(y, resid, g.reshape(1, d), b.reshape(1, d))


def _rope_freqs(dim):
    return 1.0 / (ROPE_THETA ** (jnp.arange(0, dim, 2, dtype=F32) / dim))


def _retnet_freqs(dim):
    return 1.0 / (ROPE_THETA ** jnp.linspace(0.0, 1.0, dim // 2, dtype=F32))


def _rope(x, pos, inv_freq):
    ang = pos.astype(F32)[:, None] * inv_freq[None, :]
    cos = jnp.cos(ang)[:, None, :]
    sin = jnp.sin(ang)[:, None, :]
    x1, x2 = jnp.split(x, 2, axis=-1)
    return jnp.concatenate([x1 * cos - x2 * sin, x2 * cos + x1 * sin], axis=-1)


def _complex_affine_combine(e1, e2):
    a1r, a1i, b1r, b1i = e1
    a2r, a2i, b2r, b2i = e2
    return (a2r * a1r - a2i * a1i, a2r * a1i + a2i * a1r,
            a2r * b1r - a2i * b1i + b2r, a2r * b1i + a2i * b1r + b2i)


def _ssm_mixer(x, h0, a_re, a_im, log_dt, b_re, b_im, c_re, c_im, d_skip, w_glu):
    bt, L, _ = x.shape
    u = x.reshape(bt, L, SSM_GROUPS, SSM_GROUP)
    ar = a_re.astype(F32)
    ai = a_im.astype(F32)
    dt = jnp.exp(log_dt.astype(F32))[:, None]
    mag = jnp.exp(ar * dt)
    abar_r = mag * jnp.cos(ai * dt)
    abar_i = mag * jnp.sin(ai * dt)
    den = ar * ar + ai * ai
    zr = abar_r - 1.0
    coef_r = (zr * ar + abar_i * ai) / den
    coef_i = (abar_i * ar - zr * ai) / den
    br = b_re.astype(F32)
    bi = b_im.astype(F32)
    bbar_r = coef_r[..., None] * br - coef_i[..., None] * bi
    bbar_i = coef_r[..., None] * bi + coef_i[..., None] * br
    bu_r = jnp.einsum('blgc,gnc->lbgn', u, bbar_r)
    bu_i = jnp.einsum('blgc,gnc->lbgn', u, bbar_i)
    a_seq_r = jnp.broadcast_to(abar_r[None, None], (L, 1, SSM_GROUPS, SSM_N))
    a_seq_i = jnp.broadcast_to(abar_i[None, None], (L, 1, SSM_GROUPS, SSM_N))
    acc_r, acc_i, h_r, h_i = lax.associative_scan(
        _complex_affine_combine, (a_seq_r, a_seq_i, bu_r, bu_i), axis=0)
    h0 = h0.astype(F32)
    h0r, h0i = h0[..., 0], h0[..., 1]
    h_r, h_i = (h_r + acc_r * h0r - acc_i * h0i, h_i + acc_r * h0i + acc_i * h0r)
    y = (jnp.einsum('lbgn,gcn->blgc', h_r, c_re) - jnp.einsum('lbgn,gcn->blgc', h_i, c_im)
         + u * d_skip)
    y = jax.nn.gelu(y.reshape(bt, L, D_MODEL))
    yb = y.reshape(bt * L, D_MODEL).astype(BF16)
    out = dual_mm(yb, w_glu, w_glu, combine=_glu_combine, n=D_MODEL, col0_b=D_MODEL,
                  out_dtype=F32)
    return out, jnp.stack([h_r[-1], h_i[-1]], axis=-1)


def _retention_mixer(xb, bt, L, pos0, s0, w_in, norm_g, w_out):
    H = RET_HEADS
    dqk = H * RET_DK
    dv = H * RET_DV
    proj = mm(xb, w_in).reshape(bt, L, -1)
    q, k, v, g = jnp.split(proj, [dqk, 2 * dqk, 2 * dqk + dv], axis=-1)
    pos = pos0 + jnp.arange(L)
    inv = _retnet_freqs(RET_DK)
    q = _rope(q.reshape(bt, L, H, RET_DK), pos, inv).astype(F32)
    k = (_rope(k.reshape(bt, L, H, RET_DK), pos, inv) * RET_DK ** -0.5).astype(F32)
    v = v.reshape(bt, L, H, RET_DV).astype(F32)
    C = math.gcd(L, RET_CHUNK)
    nc = L // C
    log_g = jnp.log1p(-jnp.exp2(-5.0 - jnp.arange(H, dtype=F32)))
    idx = jnp.arange(C, dtype=F32)
    diff = idx[:, None] - idx[None, :]
    intra = jnp.where(diff >= 0, jnp.exp(log_g[:, None, None] * jnp.maximum(diff, 0.0)), 0.0)
    q_dec = jnp.exp(log_g[None, :] * (idx[:, None] + 1.0))
    k_dec = jnp.exp(log_g[None, :] * (C - 1.0 - idx[:, None]))
    chunk_dec = jnp.exp(log_g * C)

    def to_chunks(t):
        return jnp.moveaxis(t.reshape(bt, nc, C, *t.shape[2:]), 1, 0)

    def step(s, inp):
        qc, kc, vc = inp
        sc = jnp.einsum('bihd,bjhd->bhij', qc, kc) * intra
        o = jnp.einsum('bhij,bjhe->bihe', sc, vc)
        o = o + jnp.einsum('bihd,bhde->bihe', qc, s) * q_dec[None, :, :, None]
        s = (s * chunk_dec[None, :, None, None]
             + jnp.einsum('bjhd,bjhe->bhde', kc * k_dec[None, :, :, None], vc))
        return s, o

    s_fin, o = lax.scan(step, s0.astype(F32), (to_chunks(q), to_chunks(k), to_chunks(v)))
    o = jnp.moveaxis(o, 0, 1).reshape(bt, L, H, RET_DV)
    mu = jnp.mean(o, -1, keepdims=True)
    var = jnp.mean(jnp.square(o - mu), -1, keepdims=True)
    o = ((o - mu) * lax.rsqrt(var + LN_EPS)).reshape(bt, L, dv) * norm_g
    ob = (jax.nn.silu(g) * o).reshape(bt * L, dv).astype(BF16)
    return mm(ob, w_out), s_fin


def _compress(blk, pe, w1, w2):
    bt, nb = blk.shape[0], blk.shape[1]
    h = jnp.moveaxis(blk + pe[None, None, :, None, :], 2, 3).reshape(bt, nb, NSA_KVH, CMP_BLOCK * NSA_DH)
    return jax.nn.gelu(h @ w1) @ w2


def _nsa_mixer(xb, bt, L, pos0, past_cmp, past_slc, win_prefix, n_keep,
               w_in_main, w_in_gate, pe_ck, w_ck1, w_ck2, pe_cv, w_cv1, w_cv2, w_out):
    G, R, dh = NSA_KVH, NSA_REP, NSA_DH
    kvw = G * dh
    qw_ = NSA_HEADS * dh
    proj = mm(xb, w_in_main).reshape(bt, L, -1)
    gates = mm(xb, w_in_gate, bn=128).reshape(bt, L, -1)[..., :3 * NSA_HEADS]
    splits = [qw_ + j * kvw for j in range(6)]
    q, kc, vc, ks, vs, kw, vw = jnp.split(proj, splits, axis=-1)
    pos = pos0 + jnp.arange(L)
    inv = _rope_freqs(dh)
    q = _rope(q.reshape(bt, L, NSA_HEADS, dh), pos, inv).reshape(bt, L, G, R, dh) * dh ** -0.5
    new_cmp = jnp.stack([kc.reshape(bt, L, G, dh), vc.reshape(bt, L, G, dh)], axis=2)
    new_slc = jnp.stack([_rope(ks.reshape(bt, L, G, dh), pos, inv), vs.reshape(bt, L, G, dh)], axis=2)
    new_win = jnp.stack([_rope(kw.reshape(bt, L, G, dh), pos, inv), vw.reshape(bt, L, G, dh)], axis=2)
    all_cmp = new_cmp if past_cmp is None else jnp.concatenate([past_cmp, new_cmp], axis=1)
    all_slc = new_slc if past_slc is None else jnp.concatenate([past_slc, new_slc], axis=1)
    T = all_cmp.shape[1]
    Tp = -(-T // SEL_BLOCK) * SEL_BLOCK
    pad = ((0, 0), (0, Tp - T), (0, 0), (0, 0), (0, 0))
    all_cmp = jnp.pad(all_cmp, pad)
    all_slc = jnp.pad(all_slc, pad)

    nb = Tp // CMP_BLOCK
    blk = all_cmp.reshape(bt, nb, CMP_BLOCK, 2, G, dh)
    blk_end = (jnp.arange(nb) + 1) * CMP_BLOCK - 1
    ck = _rope(_compress(blk[:, :, :, 0], pe_ck, w_ck1, w_ck2), blk_end, inv)
    cv = _compress(blk[:, :, :, 1], pe_cv, w_cv1, w_cv2)
    m_cmp = (blk_end[None, :] <= pos[:, None])[None, :, None, None, :]
    s_cmp = jnp.einsum('blgrd,bngd->blgrn', q, ck)
    p_cmp = jax.nn.softmax(jnp.where(m_cmp, s_cmp, NEG_INF), axis=-1) * m_cmp
    o_cmp = jnp.einsum('blgrn,bngd->blgrd', p_cmp, cv)

    nsel = Tp // SEL_BLOCK
    imp = p_cmp.sum(axis=3).reshape(bt, L, G, nsel, SEL_BLOCK // CMP_BLOCK).sum(-1)
    j = jnp.arange(nsel)[None, :]
    cur = (pos // SEL_BLOCK)[:, None]
    forced = ((j == 0) | (j == cur) | (j == cur - 1))[None, :, None, :]
    future = (j * SEL_BLOCK > pos[:, None])[None, :, None, :]
    score = jnp.where(forced, SEL_FORCE, jnp.where(future, -SEL_FORCE, imp))
    k_sel = min(N_SEL, nsel)
    _, sel_idx = lax.top_k(score, k_sel)

    ks_blk = jnp.moveaxis(all_slc[:, :, 0].reshape(bt, nsel, SEL_BLOCK, G, dh), 3, 1)
    vs_blk = jnp.moveaxis(all_slc[:, :, 1].reshape(bt, nsel, SEL_BLOCK, G, dh), 3, 1)
    b_ix = jnp.arange(bt)[:, None, None, None]
    g_ix = jnp.arange(G)[None, None, :, None]
    qb = math.gcd(L, SEL_QBLK)
    nq = L // qb

    def sel_block(args):
        qq, ii, pp = args
        kk = ks_blk[b_ix, g_ix, ii]
        vv = vs_blk[b_ix, g_ix, ii]
        s = jnp.einsum('bqgrd,bqgkjd->bqgrkj', qq, kk)
        kpos = ii[..., None] * SEL_BLOCK + jnp.arange(SEL_BLOCK)
        msk = (kpos <= pp[None, :, None, None, None])[:, :, :, None]
        s = jnp.where(msk, s, NEG_INF).reshape(bt, qb, G, R, k_sel * SEL_BLOCK)
        p = jax.nn.softmax(s, axis=-1).reshape(bt, qb, G, R, k_sel, SEL_BLOCK)
        return jnp.einsum('bqgrkj,bqgkjd->bqgrd', p, vv)

    o_slc = lax.map(sel_block, (jnp.moveaxis(q.reshape(bt, nq, qb, G, R, dh), 1, 0),
                                jnp.moveaxis(sel_idx.reshape(bt, nq, qb, G, k_sel), 1, 0),
                                pos.reshape(nq, qb)))
    o_slc = jnp.moveaxis(o_slc, 0, 1).reshape(bt, L, G, R, dh)

    P = win_prefix.shape[1]
    win_all = jnp.concatenate([win_prefix, new_win], axis=1)
    kw_all, vw_all = win_all[:, :, 0], win_all[:, :, 1]
    qwb = math.gcd(L, WIN_QBLK)
    nqw = L // qwb
    span = P + qwb

    def win_block(i):
        st = i * qwb
        qq = lax.dynamic_slice_in_dim(q, st, qwb, axis=1)
        kk = lax.dynamic_slice_in_dim(kw_all, st, span, axis=1)
        vv = lax.dynamic_slice_in_dim(vw_all, st, span, axis=1)
        qpos = pos0 + st + jnp.arange(qwb)
        kpos = pos0 - P + st + jnp.arange(span)
        msk = ((kpos[None, :] >= 0) & (kpos[None, :] <= qpos[:, None])
               & (kpos[None, :] > qpos[:, None] - WINDOW))[None, :, None, None, :]
        s = jnp.einsum('bqgrd,bkgd->bqgrk', qq, kk)
        p = jax.nn.softmax(jnp.where(msk, s, NEG_INF), axis=-1)
        return jnp.einsum('bqgrk,bkgd->bqgrd', p, vv)

    o_win = lax.map(win_block, jnp.arange(nqw))
    o_win = jnp.moveaxis(o_win, 0, 1).reshape(bt, L, G, R, dh)

    gt = jax.nn.sigmoid(gates).reshape(bt, L, G, R, 3)
    o = gt[..., 0:1] * o_cmp + gt[..., 1:2] * o_slc + gt[..., 2:3] * o_win
    ob = o.reshape(bt * L, NSA_HEADS * dh).astype(BF16)
    return mm(ob, w_out), new_cmp, new_slc, win_all[:, -n_keep:]


def _pool_mixer(x, pos0, prefix, w_pool, scale):
    bt, L, _ = x.shape
    P = prefix.shape[1]
    xx = jnp.concatenate([prefix.astype(F32), x], axis=1)
    cs = jnp.concatenate([jnp.zeros((bt, 1, D_MODEL), F32), jnp.cumsum(xx, axis=1)], axis=1)
    end = cs[:, P + 1:]
    pos = pos0 + jnp.arange(L)
    outs = []
    for gi, w in enumerate(POOL_WINDOWS):
        sl = slice(gi * POOL_CH, (gi + 1) * POOL_CH)
        start = cs[:, P + 1 - w:P + 1 - w + L, sl]
        cnt = jnp.minimum(w, pos + 1).astype(F32)[None, :, None]
        mean = (end[:, :, sl] - start) / cnt
        outs.append((mean - x[:, :, sl]) @ w_pool[gi])
    return jnp.concatenate(outs, axis=-1) * scale, xx[:, -POOL_PAST:]


def kernel(x_prompt, x_sample, state_ssm, state_ret, cache_cmp_kv, cache_slc_kv, cache_win_kv, state_pool, page_table, ln_g, ln_b, ffn_w_gate, ffn_w_up, ffn_w_down, ssm_a_re, ssm_a_im, ssm_log_dt, ssm_b_re, ssm_b_im, ssm_c_re, ssm_c_im, ssm_d, ssm_w_glu, ret_w_in, ret_norm_g, ret_w_out, nsa_w_in, nsa_pe_ck, nsa_w_ck1, nsa_w_ck2, nsa_pe_cv, nsa_w_cv1, nsa_w_cv2, nsa_w_out, pool_w, pool_scale):
    bp, seq, _ = x_prompt.shape
    bs, dseq, _ = x_sample.shape
    past_len = page_table.shape[1] * PAGE_SIZE

    wg = ffn_w_gate.astype(BF16)
    wu = ffn_w_up.astype(BF16)
    wd = ffn_w_down.astype(BF16)
    w_glu = ssm_w_glu[0].astype(BF16)
    r_w_in = ret_w_in[0].astype(BF16)
    r_w_out = ret_w_out[0].astype(BF16)
    n_main = NSA_HEADS * NSA_DH + 6 * NSA_KVH * NSA_DH
    n_w_in_main = nsa_w_in[0, :, :n_main].astype(BF16)
    n_w_in_gate = jnp.pad(nsa_w_in[0, :, n_main:], ((0, 0), (0, 128 - 3 * NSA_HEADS))).astype(BF16)
    n_w_out = nsa_w_out[0].astype(BF16)

    def ffn(xf, xb, i, s, ln_idx):
        h = dual_mm(xb, wg, wu, combine=_swiglu_combine, n=D_FF, widx_a=(i, s), widx_b=(i, s))
        y = mm(h, wd, widx=(i, s))
        return res_ln(y, xf, ln_g[i, ln_idx], ln_b[i, ln_idx], scale=0.5)

    def stream(x, pos0, is_prompt):
        bt, L, _ = x.shape
        xf = x.astype(F32).reshape(bt * L, D_MODEL)
        xb = xf.astype(BF16)
        outs = {}
        for i in range(DEPTH):
            m = i % N_MIXERS
            xf, xb = ffn(xf, xb, i, 0, 0)
            x3 = xf.reshape(bt, L, D_MODEL)
            if m == 0:
                h0 = jnp.zeros((bt, SSM_GROUPS, SSM_N, 2), F32) if is_prompt else state_ssm[0]
                y, st = _ssm_mixer(x3, h0, ssm_a_re[0], ssm_a_im[0], ssm_log_dt[0], ssm_b_re[0],
                                   ssm_b_im[0], ssm_c_re[0], ssm_c_im[0], ssm_d[0], w_glu)
                outs['ssm'] = st
            elif m == 1:
                s0 = jnp.zeros((bt, RET_HEADS, RET_DK, RET_DV), F32) if is_prompt else state_ret[0]
                y, st = _retention_mixer(xb, bt, L, pos0, s0, r_w_in, ret_norm_g[0], r_w_out)
                outs['ret'] = st
            elif m == 2:
                if is_prompt:
                    past_c = past_s = None
                    win_prefix = jnp.zeros((bt, WINDOW, 2, NSA_KVH, NSA_DH), F32)
                    n_keep = min(WINDOW, L)
                else:
                    past_c = cache_cmp_kv[0][page_table].reshape(bt, past_len, 2, NSA_KVH, NSA_DH)
                    past_s = cache_slc_kv[0][page_table].reshape(bt, past_len, 2, NSA_KVH, NSA_DH)
                    win_prefix = cache_win_kv[0]
                    n_keep = cache_win_kv.shape[2]
                y, c_new, s_new, w_new = _nsa_mixer(
                    xb, bt, L, pos0, past_c, past_s, win_prefix, n_keep, n_w_in_main, n_w_in_gate,
                    nsa_pe_ck[0], nsa_w_ck1[0], nsa_w_ck2[0], nsa_pe_cv[0], nsa_w_cv1[0],
                    nsa_w_cv2[0], n_w_out)
                outs['cmp'], outs['slc'], outs['win'] = c_new, s_new, w_new
            else:
                prefix = jnp.zeros((bt, POOL_PAST, D_MODEL), F32) if is_prompt else state_pool[0]
                y, st = _pool_mixer(x3, pos0, prefix, pool_w[0], pool_scale[0])
                y = y.reshape(bt * L, D_MODEL)
                outs['pool'] = st
            xf, xb = res_ln(y, xf, ln_g[i, 1], ln_b[i, 1], scale=1.0)
            xf, xb = ffn(xf, xb, i, 1, 2)
        return xf.reshape(bt, L, D_MODEL), outs

    yp, op = stream(x_prompt, 0, True)
    ys, os_ = stream(x_sample, past_len, False)
    return (yp.astype(x_prompt.dtype), ys.astype(x_sample.dtype),
            op['ssm'][None], os_['ssm'][None], op['ret'][None], os_['ret'][None],
            op['cmp'][None], os_['cmp'][None], op['slc'][None], os_['slc'][None],
            op['win'][None], os_['win'][None], op['pool'][None], os_['pool'][None])
```

```python
import functools
import math

import jax
import jax.numpy as jnp
from jax import lax
from jax.experimental import pallas as pl
from jax.experimental.pallas import tpu as pltpu

F32 = jnp.float32
BF16 = jnp.bfloat16

D_MODEL = 4096
DEPTH = 4
PAGE_SIZE = 128
N_MIXERS = 4
D_FF = 2 * D_MODEL
DN_ALPHA = (2.0 * DEPTH) ** 0.25
LN_EPS = 1e-5
ROPE_THETA = 10000.0
NEG_INF = -1e30

SSM_GROUP = 16
SSM_GROUPS = D_MODEL // SSM_GROUP
SSM_N = 64
SSM_TILE_GROUPS = 16
SSM_TILE_CH = SSM_TILE_GROUPS * SSM_GROUP
SSM_TILE_ST = SSM_TILE_GROUPS * SSM_N

RET_DK = 256
RET_HEADS = D_MODEL // RET_DK
RET_DV = 2 * RET_DK
RET_CHUNK = 128

NSA_DH = 128
NSA_HEADS = D_MODEL // NSA_DH
NSA_KVH = 4
NSA_REP = NSA_HEADS // NSA_KVH
CMP_BLOCK = 32
SEL_BLOCK = 64
N_SEL = 16
WINDOW = 512
SEL_FORCE = 1e4
NSA_KV_W = NSA_KVH * NSA_DH
NSA_ROW_W = 2 * NSA_KV_W
NSA_TQ = 128

POOL_WINDOWS = (2, 4, 8, 16)
POOL_CH = D_MODEL // len(POOL_WINDOWS)
POOL_PAST = max(POOL_WINDOWS) - 1
POOL_HALO = 16

LANES = 128
V7X_VMEM_BYTES = 64 * 1024 * 1024
VMEM_LIMIT = V7X_VMEM_BYTES - 8 * 1024 * 1024


def _cparams(*sem):
    return pltpu.CompilerParams(dimension_semantics=sem, vmem_limit_bytes=VMEM_LIMIT)


def _dot(a, b):
    return jnp.dot(a, b, preferred_element_type=F32)


def _dot_nt(a, b):
    return lax.dot_general(a, b, (((1,), (1,)), ((), ())), preferred_element_type=F32)


def _dot_tn(a, b):
    return lax.dot_general(a, b, (((0,), (0,)), ((), ())), preferred_element_type=F32)


def _mm_kernel(x_ref, w_ref, o_ref):
    o_ref[...] = _dot(x_ref[...], w_ref[...]).astype(o_ref.dtype)


def _w_spec(k, bn, col0_blocks, widx):
    lead = tuple(widx)
    shape = (None,) * len(lead) + (k, bn)
    return pl.BlockSpec(shape, lambda i, j: lead + (0, j + col0_blocks))


def mm(x, w, *, widx=(), n=None, col0=0, out_dtype=F32, bm=None, bn=512):
    m, k = x.shape
    n = w.shape[-1] if n is None else n
    if bm is None:
        bm = min(m, 1024 if k <= 4096 else 512)
    bn = min(bn, n)
    assert m % bm == 0 and n % bn == 0 and col0 % bn == 0
    return pl.pallas_call(
        _mm_kernel,
        grid=(m // bm, n // bn),
        in_specs=[pl.BlockSpec((bm, k), lambda i, j: (i, 0)),
                  _w_spec(k, bn, col0 // bn, widx)],
        out_specs=pl.BlockSpec((bm, bn), lambda i, j: (i, j)),
        out_shape=jax.ShapeDtypeStruct((m, n), out_dtype),
        compiler_params=_cparams("parallel", "arbitrary"),
        name="mm",
    )(x, w)


def _dual_mm_kernel(x_ref, wa_ref, wb_ref, o_ref, *, combine):
    x = x_ref[...]
    o_ref[...] = combine(_dot(x, wa_ref[...]), _dot(x, wb_ref[...])).astype(o_ref.dtype)


def _swiglu_combine(a, b):
    return a * jax.nn.sigmoid(a) * b


def _glu_combine(a, b):
    return a * jax.nn.sigmoid(b)


def dual_mm(x, wa, wb, *, combine, n, widx_a=(), widx_b=(), col0_a=0, col0_b=0,
            out_dtype=BF16, bm=None, bn=512):
    m, k = x.shape
    if bm is None:
        bm = min(m, 1024)
    assert m % bm == 0 and n % bn == 0 and col0_a % bn == 0 and col0_b % bn == 0
    return pl.pallas_call(
        functools.partial(_dual_mm_kernel, combine=combine),
        grid=(m // bm, n // bn),
        in_specs=[pl.BlockSpec((bm, k), lambda i, j: (i, 0)),
                  _w_spec(k, bn, col0_a // bn, widx_a),
                  _w_spec(k, bn, col0_b // bn, widx_b)],
        out_specs=pl.BlockSpec((bm, bn), lambda i, j: (i, j)),
        out_shape=jax.ShapeDtypeStruct((m, n), out_dtype),
        compiler_params=_cparams("parallel", "arbitrary"),
        name="dual_mm",
    )(x, wa, wb)


def _res_ln_kernel(y_ref, r_ref, g_ref, b_ref, of_ref, ob_ref, *, scale):
    z = DN_ALPHA * r_ref[...] + scale * y_ref[...].astype(F32)
    mu = jnp.mean(z, axis=-1, keepdims=True)
    zc = z - mu
    var = jnp.mean(zc * zc, axis=-1, keepdims=True)
    out = zc * lax.rsqrt(var + LN_EPS) * g_ref[...] + b_ref[...]
    of_ref[...] = out
    ob_ref[...] = out.astype(BF16)


def res_ln(y, resid, g, b, *, scale=1.0, bt=1, in_tm=False, out_tm=False, bm=256):
    m, d = resid.shape
    L = m // bt
    bm = min(bm, L)
    assert L % bm == 0
    nl = L // bm
    if (in_tm or out_tm) and bm % 8:
        raise ValueError("time-major blocks need a sublane-aligned row block")
    tok = pl.BlockSpec((bm, d), lambda bi, li: (bi * nl + li, 0))
    tmj = pl.BlockSpec((bm, d), lambda bi, li: (li, bi))
    vec = pl.BlockSpec((1, d), lambda bi, li: (0, 0))
    in_spec = tmj if in_tm else tok
    out_spec = tmj if out_tm else tok
    out_shape = (L, bt * d) if out_tm else (m, d)
    if in_tm:
        y = y.reshape(L, bt * d)
        resid = resid.reshape(L, bt * d)
    of, ob = pl.pallas_call(
        functools.partial(_res_ln_kernel, scale=scale),
        grid=(bt, nl),
        in_specs=[in_spec, in_spec, vec, vec],
        out_specs=[out_spec, out_spec],
        out_shape=[jax.ShapeDtypeStruct(out_shape, F32), jax.ShapeDtypeStruct(out_shape, BF16)],
        compiler_params=_cparams("parallel", "parallel"),
        name="res_ln",
    )(y, resid, g.reshape(1, d), b.reshape(1, d))
    return of.reshape(m, d), ob.reshape(m, d)


def _ssm_kernel(u_ref, h0_ref, a_ref, bblk_ref, cblk_ref, d_ref, y_ref, hout_ref,
                h_scr, bu_scr, hs_scr, *, bt, tc, nchunks):
    c = pl.program_id(1)
    st = SSM_TILE_ST

    @pl.when(c == 0)
    def _():
        h_scr[...] = h0_ref[...]

    u = u_ref[...]
    bu_scr[...] = _dot(u.astype(BF16), bblk_ref[...])
    ar = a_ref[0:1, :]
    ai = a_ref[1:2, :]
    spi = 8 // bt

    def step(t8, carry):
        hr, hi = carry
        r0 = pl.multiple_of(t8 * 8, 8)
        bur = bu_scr[pl.ds(r0, 8), 0:st]
        bui = bu_scr[pl.ds(r0, 8), st:2 * st]
        outs_r, outs_i = [], []
        for s in range(spi):
            nhr = ar * hr - ai * hi + bur[s * bt:(s + 1) * bt]
            nhi = ar * hi + ai * hr + bui[s * bt:(s + 1) * bt]
            hr, hi = nhr, nhi
            outs_r.append(hr)
            outs_i.append(hi)
        hs_scr[pl.ds(r0, 8), 0:st] = outs_r[0] if spi == 1 else jnp.concatenate(outs_r, axis=0)
        hs_scr[pl.ds(r0, 8), st:2 * st] = outs_i[0] if spi == 1 else jnp.concatenate(outs_i, axis=0)
        return hr, hi

    hr, hi = lax.fori_loop(0, tc // spi, step, (h_scr[:, 0:st], h_scr[:, st:2 * st]))
    h_scr[:, 0:st] = hr
    h_scr[:, st:2 * st] = hi
    y = _dot(hs_scr[...].astype(BF16), cblk_ref[...]) + u * d_ref[...]
    y_ref[...] = jax.nn.gelu(y).astype(y_ref.dtype)

    @pl.when(c == nchunks - 1)
    def _():
        hout_ref[...] = h_scr[...]


def _ssm_tables(a_re, a_im, log_dt, b_re, b_im, c_re, c_im):
    ar = a_re.astype(F32)
    ai = a_im.astype(F32)
    dt = jnp.exp(log_dt.astype(F32))[:, None]
    mag = jnp.exp(ar * dt)
    abar_r = mag * jnp.cos(ai * dt)
    abar_i = mag * jnp.sin(ai * dt)
    den = ar * ar + ai * ai
    zr = abar_r - 1.0
    coef_r = (zr * ar + abar_i * ai) / den
    coef_i = (abar_i * ar - zr * ai) / den
    br = b_re.astype(F32)
    bi = b_im.astype(F32)
    bbar_r = coef_r[..., None] * br - coef_i[..., None] * bi
    bbar_i = coef_r[..., None] * bi + coef_i[..., None] * br
    gt = SSM_TILE_GROUPS
    nt = SSM_GROUPS // gt
    eye = jnp.eye(gt, dtype=F32)
    bb = jnp.stack([bbar_r, bbar_i]).reshape(2, nt, gt, SSM_N, SSM_GROUP)
    bblk = jnp.einsum('rtgnc,gh->tgcrhn', bb, eye).reshape(nt, SSM_TILE_CH, 2 * SSM_TILE_ST)
    cc = jnp.stack([c_re.astype(F32), -c_im.astype(F32)]).reshape(2, nt, gt, SSM_GROUP, SSM_N)
    cblk = jnp.einsum('rtgcn,gh->trgnhc', cc, eye).reshape(nt, 2 * SSM_TILE_ST, SSM_TILE_CH)
    a_tab = jnp.stack([abar_r.reshape(nt, SSM_TILE_ST), abar_i.reshape(nt, SSM_TILE_ST)], axis=1)
    return a_tab, bblk.astype(BF16), cblk.astype(BF16)


def ssm_scan(u_tm, h0, a_tab, bblk, cblk, d_skip, *, bt):
    m, d = u_tm.shape
    L = m // bt
    assert 8 % bt == 0 or bt == 8
    nt = SSM_GROUPS // SSM_TILE_GROUPS
    tc = min(L, 128)
    assert L % tc == 0 and (tc * bt) % 8 == 0
    nchunks = L // tc
    rows = tc * bt
    h0t = h0.astype(F32).reshape(bt, nt, SSM_TILE_GROUPS, SSM_N, 2)
    h0t = jnp.transpose(h0t, (1, 0, 4, 2, 3)).reshape(nt, bt, 2 * SSM_TILE_ST)
    y, hout = pl.pallas_call(
        functools.partial(_ssm_kernel, bt=bt, tc=tc, nchunks=nchunks),
        grid=(nt, nchunks),
        in_specs=[
            pl.BlockSpec((rows, SSM_TILE_CH), lambda j, c: (c, j)),
            pl.BlockSpec((None, bt, 2 * SSM_TILE_ST), lambda j, c: (j, 0, 0)),
            pl.BlockSpec((None, 2, SSM_TILE_ST), lambda j, c: (j, 0, 0)),
            pl.BlockSpec((None, SSM_TILE_CH, 2 * SSM_TILE_ST), lambda j, c: (j, 0, 0)),
            pl.BlockSpec((None, 2 * SSM_TILE_ST, SSM_TILE_CH), lambda j, c: (j, 0, 0)),
            pl.BlockSpec((1, SSM_TILE_CH), lambda j, c: (0, j)),
        ],
        out_specs=[
            pl.BlockSpec((rows, SSM_TILE_CH), lambda j, c: (c, j)),
            pl.BlockSpec((None, bt, 2 * SSM_TILE_ST), lambda j, c: (j, 0, 0)),
        ],
        out_shape=[jax.ShapeDtypeStruct((m, d), BF16),
                   jax.ShapeDtypeStruct((nt, bt, 2 * SSM_TILE_ST), F32)],
        scratch_shapes=[pltpu.VMEM((bt, 2 * SSM_TILE_ST), F32),
                        pltpu.VMEM((rows, 2 * SSM_TILE_ST), F32),
                        pltpu.VMEM((rows, 2 * SSM_TILE_ST), F32)],
        compiler_params=_cparams("parallel", "arbitrary"),
        name="ssm_scan",
    )(u_tm, h0t, a_tab, bblk, cblk, d_skip.astype(F32).reshape(1, d))
    hfin = hout.reshape(nt, bt, 2, SSM_TILE_GROUPS, SSM_N)
    hfin = jnp.transpose(hfin, (1, 0, 3, 4, 2)).reshape(bt, SSM_GROUPS, SSM_N, 2)
    return y, hfin


def _ret_kernel(*refs, nc, zero_init):
    if zero_init:
        (q_ref, k_ref, v_ref, g_ref, cos_ref, sin_ref, intra_ref, qd_ref, kd_ref, cd_ref, ng_ref,
         o_ref, sfin_ref, s_scr) = refs
        s0_ref = None
    else:
        (q_ref, k_ref, v_ref, g_ref, cos_ref, sin_ref, intra_ref, qd_ref, kd_ref, cd_ref, ng_ref,
         s0_ref, o_ref, sfin_ref, s_scr) = refs
    c = pl.program_id(2)

    @pl.when(c == 0)
    def _():
        if zero_init:
            s_scr[...] = jnp.zeros_like(s_scr)
        else:
            s_scr[...] = s0_ref[...]

    cos = cos_ref[...]
    sin = sin_ref[...]
    half = RET_DK // 2

    def rope(x):
        x1 = x[:, :half]
        x2 = x[:, half:]
        return jnp.concatenate([x1 * cos - x2 * sin, x2 * cos + x1 * sin], axis=1)

    q = rope(q_ref[...])
    k = rope(k_ref[...]) * RET_DK ** -0.5
    qb = q.astype(BF16)
    vb = v_ref[...].astype(BF16)
    sc = _dot_nt(qb, k.astype(BF16)) * intra_ref[...]
    s_old = s_scr[...]
    o = _dot(sc.astype(BF16), vb) + _dot(qb, s_old.astype(BF16)) * qd_ref[...]
    s_new = s_old * cd_ref[...] + _dot_tn((k * kd_ref[...]).astype(BF16), vb)
    s_scr[...] = s_new
    mu = jnp.mean(o, axis=-1, keepdims=True)
    oc = o - mu
    var = jnp.mean(oc * oc, axis=-1, keepdims=True)
    on = oc * lax.rsqrt(var + LN_EPS) * ng_ref[...]
    g = g_ref[...]
    o_ref[...] = (g * jax.nn.sigmoid(g) * on).astype(o_ref.dtype)

    @pl.when(c == nc - 1)
    def _():
        sfin_ref[...] = s_new


def retention_core(proj, s0, norm_g, *, bt, pos0):
    m, n = proj.shape
    L = m // bt
    H = RET_HEADS
    C = math.gcd(L, RET_CHUNK)
    nc = L // C
    pos = (pos0 + jnp.arange(L)).astype(F32)
    inv = 1.0 / (ROPE_THETA ** jnp.linspace(0.0, 1.0, RET_DK // 2, dtype=F32))
    ang = pos[:, None] * inv[None, :]
    cos_t, sin_t = jnp.cos(ang), jnp.sin(ang)
    log_g = jnp.log1p(-jnp.exp2(-5.0 - jnp.arange(H, dtype=F32)))
    idx = jnp.arange(C, dtype=F32)
    diff = idx[:, None] - idx[None, :]
    intra = jnp.where(diff >= 0, jnp.exp(log_g[:, None, None] * jnp.maximum(diff, 0.0)), 0.0)
    q_dec = jnp.exp(log_g[:, None] * (idx[None, :] + 1.0))[..., None]
    k_dec = jnp.exp(log_g[:, None] * (C - 1.0 - idx[None, :]))[..., None]
    chunk_dec = jnp.exp(log_g * C).reshape(H, 1, 1)
    zero_init = s0 is None
    proj3 = proj.reshape(bt, L, n)
    kb0 = H
    vb0 = 2 * H * RET_DK // RET_DV
    gb0 = vb0 + H
    in_specs = [
        pl.BlockSpec((None, C, RET_DK), lambda b, h, c: (b, c, h)),
        pl.BlockSpec((None, C, RET_DK), lambda b, h, c: (b, c, kb0 + h)),
        pl.BlockSpec((None, C, RET_DV), lambda b, h, c: (b, c, vb0 + h)),
        pl.BlockSpec((None, C, RET_DV), lambda b, h, c: (b, c, gb0 + h)),
        pl.BlockSpec((C, RET_DK // 2), lambda b, h, c: (c, 0)),
        pl.BlockSpec((C, RET_DK // 2), lambda b, h, c: (c, 0)),
        pl.BlockSpec((None, C, C), lambda b, h, c: (h, 0, 0)),
        pl.BlockSpec((None, C, 1), lambda b, h, c: (h, 0, 0)),
        pl.BlockSpec((None, C, 1), lambda b, h, c: (h, 0, 0)),
        pl.BlockSpec((None, 1, 1), lambda b, h, c: (h, 0, 0)),
        pl.BlockSpec((1, RET_DV), lambda b, h, c: (0, h)),
    ]
    args = [proj3, proj3, proj3, proj3, cos_t, sin_t, intra, q_dec, k_dec, chunk_dec,
            norm_g.astype(F32).reshape(1, H * RET_DV)]
    st_spec = pl.BlockSpec((None, None, RET_DK, RET_DV), lambda b, h, c: (b, h, 0, 0))
    if not zero_init:
        in_specs.append(st_spec)
        args.append(s0.astype(F32))
    o, sfin = pl.pallas_call(
        functools.partial(_ret_kernel, nc=nc, zero_init=zero_init),
        grid=(bt, H, nc),
        in_specs=in_specs,
        out_specs=[pl.BlockSpec((None, C, RET_DV), lambda b, h, c: (b, c, h)), st_spec],
        out_shape=[jax.ShapeDtypeStruct((bt, L, H * RET_DV), BF16),
                   jax.ShapeDtypeStruct((bt, H, RET_DK, RET_DV), F32)],
        scratch_shapes=[pltpu.VMEM((RET_DK, RET_DV), F32)],
        compiler_params=_cparams("parallel", "parallel", "arbitrary"),
        name="retention_core",
    )(*args)
    return o.reshape(m, H * RET_DV), sfin


def _pool_kernel(*refs, tile, pos0, has_halo):
    if has_halo:
        x_ref, halo_ref, pre_ref, w_ref, sc_ref, y_ref, xx_scr = refs
    else:
        x_ref, pre_ref, w_ref, sc_ref, y_ref, xx_scr = refs
        halo_ref = None
    i = pl.program_id(1)
    hl = POOL_HALO
    if has_halo:
        @pl.when(i == 0)
        def _():
            xx_scr[0:hl, :] = pre_ref[...]

        @pl.when(i > 0)
        def _():
            xx_scr[0:hl, :] = halo_ref[...]
    else:
        xx_scr[0:hl, :] = pre_ref[...]
    xx_scr[hl:hl + tile, :] = x_ref[...]
    pos = pos0 + i * tile + lax.broadcasted_iota(jnp.int32, (tile, 1), 0)
    for gi, w in enumerate(POOL_WINDOWS):
        c0, c1 = gi * POOL_CH, (gi + 1) * POOL_CH
        xs = xx_scr[hl:hl + tile, c0:c1]
        acc = xs
        for k in range(1, w):
            acc = acc + xx_scr[hl - k:hl - k + tile, c0:c1]
        cnt = jnp.minimum(w, pos + 1).astype(F32)
        dlt = (acc / cnt - xs).astype(BF16)
        y_ref[:, c0:c1] = _dot(dlt, w_ref[gi]) * sc_ref[:, c0:c1]


def pool_core(xf, prefix, w_pool, scale, *, bt, pos0):
    m, d = xf.shape
    L = m // bt
    tile = min(L, 256)
    assert L % tile == 0
    has_halo = L > tile
    x3 = xf.reshape(bt, L, d)
    pre = jnp.pad(prefix.astype(F32), ((0, 0), (POOL_HALO - POOL_PAST, 0), (0, 0)))
    hb = tile // POOL_HALO
    row = pl.BlockSpec((None, tile, d), lambda b, i: (b, i, 0))
    in_specs = [row]
    args = [x3]
    if has_halo:
        in_specs.append(pl.BlockSpec((None, POOL_HALO, d), lambda b, i: (b, jnp.maximum(i * hb - 1, 0), 0)))
        args.append(x3)
    in_specs += [
        pl.BlockSpec((None, POOL_HALO, d), lambda b, i: (b, 0, 0)),
        pl.BlockSpec((len(POOL_WINDOWS), POOL_CH, POOL_CH), lambda b, i: (0, 0, 0)),
        pl.BlockSpec((1, d), lambda b, i: (0, 0)),
    ]
    args += [pre, w_pool, scale.astype(F32).reshape(1, d)]
    y = pl.pallas_call(
        functools.partial(_pool_kernel, tile=tile, pos0=pos0, has_halo=has_halo),
        grid=(bt, L // tile),
        in_specs=in_specs,
        out_specs=row,
        out_shape=jax.ShapeDtypeStruct((bt, L, d), F32),
        scratch_shapes=[pltpu.VMEM((POOL_HALO + tile, d), F32)],
        compiler_params=_cparams("parallel", "arbitrary"),
        name="pool_core",
    )(*args)
    return y.reshape(m, d)


def _rope_tables(pos, dim):
    inv = 1.0 / (ROPE_THETA ** (jnp.arange(0, dim, 2, dtype=F32) / dim))
    ang = pos.astype(F32)[:, None] * inv[None, :]
    cos = jnp.cos(ang)
    sin = jnp.sin(ang)
    return jnp.concatenate([cos, cos], axis=1), jnp.concatenate([-sin, sin], axis=1)


def _rope128(x, cos, sin):
    return x * cos + pltpu.roll(x, NSA_DH // 2, 1) * sin


def _nsa_prep_kernel(p_ref, cos_ref, sin_ref, q_ref, cmp_ref, slc_ref, win_ref):
    cos = cos_ref[...]
    sin = sin_ref[...]
    dh = NSA_DH
    qw = NSA_HEADS * dh
    for h in range(NSA_HEADS):
        x = p_ref[:, h * dh:(h + 1) * dh]
        q_ref[:, h * dh:(h + 1) * dh] = (_rope128(x, cos, sin) * dh ** -0.5).astype(q_ref.dtype)
    kvw = NSA_KV_W
    cmp_ref[...] = p_ref[:, qw:qw + 2 * kvw]
    for dst, base in ((slc_ref, qw + 2 * kvw), (win_ref, qw + 4 * kvw)):
        for g in range(NSA_KVH):
            dst[:, g * dh:(g + 1) * dh] = _rope128(p_ref[:, base + g * dh:base + (g + 1) * dh], cos, sin)
        dst[:, kvw:2 * kvw] = p_ref[:, base + kvw:base + 2 * kvw]


def nsa_prep(proj, cos, sin, *, rows_per_table):
    m, n = proj.shape
    bm = min(m, 256)
    assert m % bm == 0 and rows_per_table % bm == 0
    nt = rows_per_table // bm
    row = lambda w: pl.BlockSpec((bm, w), lambda i: (i, 0))
    tab = pl.BlockSpec((bm, NSA_DH), lambda i: (i % nt, 0))
    return pl.pallas_call(
        _nsa_prep_kernel,
        grid=(m // bm,),
        in_specs=[row(n), tab, tab],
        out_specs=[row(NSA_HEADS * NSA_DH), row(NSA_ROW_W), row(NSA_ROW_W), row(NSA_ROW_W)],
        out_shape=[jax.ShapeDtypeStruct((m, NSA_HEADS * NSA_DH), BF16)]
        + [jax.ShapeDtypeStruct((m, NSA_ROW_W), F32)] * 3,
        compiler_params=_cparams("parallel"),
        name="nsa_prep",
    )(proj, cos, sin)


def _cmp_relayout_kernel(pt_ref, src_ref, out_ref):
    del pt_ref
    per_page = PAGE_SIZE // CMP_BLOCK
    for kv in range(2):
        for g in range(NSA_KVH):
            c0 = kv * NSA_KV_W + g * NSA_DH
            out_ref[kv, g] = src_ref[:, c0:c0 + NSA_DH].reshape(per_page, CMP_BLOCK, NSA_DH)


def cmp_relayout(src_pages, page_table):
    bt, n_pages = page_table.shape
    per_page = PAGE_SIZE // CMP_BLOCK
    nb = n_pages * per_page
    out = pl.pallas_call(
        _cmp_relayout_kernel,
        grid_spec=pltpu.PrefetchScalarGridSpec(
            num_scalar_prefetch=1,
            grid=(bt, n_pages),
            in_specs=[pl.BlockSpec((None, PAGE_SIZE, NSA_ROW_W), lambda b, p, pt: (pt[b, p], 0, 0))],
            out_specs=pl.BlockSpec((2, NSA_KVH, None, per_page, CMP_BLOCK, NSA_DH),
                                   lambda b, p, pt: (0, 0, b, p, 0, 0)),
        ),
        out_shape=jax.ShapeDtypeStruct((2, NSA_KVH, bt, nb, CMP_BLOCK, NSA_DH), F32),
        compiler_params=_cparams("parallel", "parallel"),
        name="cmp_relayout",
    )(page_table, src_pages)
    return out.reshape(2, NSA_KVH, bt * nb, CMP_BLOCK * NSA_DH)


def _compress_kernel(x_ref, pe_ref, w1_ref, w2_ref, cos_ref, sin_ref, o_ref):
    h = (x_ref[...] + pe_ref[...]).astype(BF16)
    a = jax.nn.gelu(_dot(h, w1_ref[...]))
    y = _dot(a.astype(BF16), w2_ref[...])
    o_ref[...] = _rope128(y, cos_ref[...], sin_ref[...])


def compress(xc, pe, w1, w2, cos, sin):
    _, g, r, kdim = xc.shape
    bm = min(r, 512)
    assert r % bm == 0
    return pl.pallas_call(
        _compress_kernel,
        grid=(2, g, r // bm),
        in_specs=[
            pl.BlockSpec((None, None, bm, kdim), lambda kv, gi, i: (kv, gi, i, 0)),
            pl.BlockSpec((None, 1, kdim), lambda kv, gi, i: (kv, 0, 0)),
            pl.BlockSpec((None, kdim, NSA_DH), lambda kv, gi, i: (kv, 0, 0)),
            pl.BlockSpec((None, NSA_DH, NSA_DH), lambda kv, gi, i: (kv, 0, 0)),
            pl.BlockSpec((None, bm, NSA_DH), lambda kv, gi, i: (kv, i, 0)),
            pl.BlockSpec((None, bm, NSA_DH), lambda kv, gi, i: (kv, i, 0)),
        ],
        out_specs=pl.BlockSpec((None, None, bm, NSA_DH), lambda kv, gi, i: (kv, gi, i, 0)),
        out_shape=jax.ShapeDtypeStruct((2, g, r, NSA_DH), F32),
        compiler_params=_cparams("parallel", "parallel", "parallel"),
        name="compress",
    )(xc, pe, w1, w2, cos, sin)


def _cmp_branch(s, cv_list, row_groups, pos_rows, nbl):
    blk_end = (lax.broadcasted_iota(jnp.int32, (1, nbl), 1) + 1) * CMP_BLOCK - 1
    m_cmp = blk_end <= pos_rows
    s = jnp.where(m_cmp, s, NEG_INF)
    mx = jnp.max(s, axis=1, keepdims=True)
    e = jnp.where(m_cmp, jnp.exp(s - mx), 0.0)
    den = jnp.sum(e, axis=1, keepdims=True)
    p = e / jnp.where(den > 0.0, den, 1.0)
    pb = p.astype(BF16)
    outs = [_dot(pb[r0:r1], cv) for (r0, r1), cv in zip(row_groups, cv_list)]
    o = outs[0] if len(outs) == 1 else jnp.concatenate(outs, axis=0)
    return p, o


def _select_blocks(imp, pos_l, nsel):
    rows, nbl = imp.shape
    imp2 = imp + pltpu.roll(imp, nbl - 1, 1)
    lane = lax.broadcasted_iota(jnp.int32, (rows, nbl), 1)
    j = lane >> 1
    valid = jnp.where((lane & 1) == 0, j, nsel) < nsel
    cur = pos_l >> 6
    forced = jnp.where(j == 0, 1, jnp.where(j == cur, 1, jnp.where(j == cur - 1, 1, 0))) > 0
    future = j * SEL_BLOCK > pos_l
    score = jnp.where(forced, SEL_FORCE, jnp.where(future, -SEL_FORCE, imp2))
    score = jnp.where(valid, score, -3.0e38)
    rank = jnp.zeros((rows, nbl), F32)
    for i in range(nsel):
        col = score[:, 2 * i:2 * i + 1]
        tie = jnp.where(lane > 2 * i, 1.0, 0.0)
        rank = rank + jnp.where(col > score, 1.0, jnp.where(col == score, tie, 0.0))
    k_sel = min(N_SEL, nsel)
    return jnp.where(valid, jnp.where(rank < k_sel, 1.0, 0.0), 0.0)


def _expand_sel(sel_b, k0, tk, nbl):
    n_lane = lax.broadcasted_iota(jnp.int32, (nbl, tk), 0)
    key = k0 + lax.broadcasted_iota(jnp.int32, (nbl, tk), 1)
    e = jnp.where(n_lane == 2 * (key >> 6), 1.0, 0.0).astype(BF16)
    return _dot(sel_b, e)


def _nsa_prompt_kernel(q_ref, ck_ref, cv_ref, ks_ref, vs_ref, kw_ref, vw_ref, gate_ref, o_ref,
                       *, seq, nsel, tk):
    tq, rep, dh = NSA_TQ, NSA_REP, NSA_DH
    i = pl.program_id(2)
    l0 = i * tq
    qb = q_ref[...]
    q = jnp.concatenate([qb[:, r * dh:(r + 1) * dh] for r in range(rep)], axis=0)
    row = lax.broadcasted_iota(jnp.int32, (rep * tq, 1), 0)
    pos_rows = l0 + (row & (tq - 1))
    pos_l = l0 + lax.broadcasted_iota(jnp.int32, (tq, 1), 0)

    ck = ck_ref[...].astype(BF16)
    cv = cv_ref[...].astype(BF16)
    nbl = ck.shape[0]
    p, o_cmp = _cmp_branch(_dot_nt(q, ck), [cv], [(0, rep * tq)], pos_rows, nbl)
    imp = sum(p[r * tq:(r + 1) * tq] for r in range(rep))
    sel_b = _select_blocks(imp, pos_l, nsel).astype(BF16)

    def slc_step(t, carry):
        m_i, l_i, acc = carry
        k0 = pl.multiple_of(t * tk, tk)
        k = ks_ref[pl.ds(k0, tk), :].astype(BF16)
        v = vs_ref[pl.ds(k0, tk), :].astype(BF16)
        key = k0 + lax.broadcasted_iota(jnp.int32, (1, tk), 1)
        selexp = _expand_sel(sel_b, k0, tk, nbl)
        bias = jnp.where(jnp.where(key <= pos_l, selexp, 0.0) > 0.5, 0.0, NEG_INF)
        s = _dot_nt(q, k) + jnp.concatenate([bias] * rep, axis=0)
        m_new = jnp.maximum(m_i, jnp.max(s, axis=1, keepdims=True))
        alpha = jnp.exp(m_i - m_new)
        pe = jnp.exp(s - m_new)
        l_new = alpha * l_i + jnp.sum(pe, axis=1, keepdims=True)
        acc = alpha * acc + _dot(pe.astype(BF16), v)
        return m_new, l_new, acc

    n_kt = (l0 + tq + tk - 1) // tk
    init = (jnp.full((rep * tq, 1), NEG_INF, F32), jnp.zeros((rep * tq, 1), F32),
            jnp.zeros((rep * tq, dh), F32))
    _, l_f, acc = lax.fori_loop(0, n_kt, slc_step, init)
    o_slc = acc / l_f

    span = min(seq, WINDOW + tq)
    start = pl.multiple_of(jnp.maximum(l0 - WINDOW, 0), tq)
    kw = kw_ref[pl.ds(start, span), :].astype(BF16)
    vw = vw_ref[pl.ds(start, span), :].astype(BF16)
    key = start + lax.broadcasted_iota(jnp.int32, (1, span), 1)
    okw = jnp.where(key <= pos_l, jnp.where(key > pos_l - WINDOW, 1.0, 0.0), 0.0)
    bias = jnp.where(okw > 0.5, 0.0, NEG_INF)
    s = _dot_nt(q, kw) + jnp.concatenate([bias] * rep, axis=0)
    pw = jnp.exp(s - jnp.max(s, axis=1, keepdims=True))
    o_win = _dot(pw.astype(BF16), vw) / jnp.sum(pw, axis=1, keepdims=True)

    gt = jax.nn.sigmoid(gate_ref[...])
    for r in range(rep):
        rs = slice(r * tq, (r + 1) * tq)
        o_r = (gt[:, 3 * r:3 * r + 1] * o_cmp[rs] + gt[:, 3 * r + 1:3 * r + 2] * o_slc[rs]
               + gt[:, 3 * r + 2:3 * r + 3] * o_win[rs])
        o_ref[:, r * dh:(r + 1) * dh] = o_r.astype(o_ref.dtype)


def nsa_attn_prompt(q, ckv, new_slc, new_win, gates_g, *, bt, nsel):
    m = q.shape[0]
    L = m // bt
    tq = NSA_TQ
    assert L % tq == 0 and L >= WINDOW + tq
    nq = L // tq
    nbl = ckv.shape[3]
    tk = 512
    gw = NSA_REP * NSA_DH
    slc3 = new_slc.reshape(bt, L, NSA_ROW_W)
    win3 = new_win.reshape(bt, L, NSA_ROW_W)
    kv_spec = lambda off: pl.BlockSpec((None, L, NSA_DH), lambda b, g, i: (b, 0, off + g))
    return pl.pallas_call(
        functools.partial(_nsa_prompt_kernel, seq=L, nsel=nsel, tk=tk),
        grid=(bt, NSA_KVH, nq),
        in_specs=[
            pl.BlockSpec((tq, gw), lambda b, g, i: (b * nq + i, g)),
            pl.BlockSpec((None, None, None, nbl, NSA_DH), lambda b, g, i: (0, g, b, 0, 0)),
            pl.BlockSpec((None, None, None, nbl, NSA_DH), lambda b, g, i: (1, g, b, 0, 0)),
            kv_spec(0), kv_spec(NSA_KVH), kv_spec(0), kv_spec(NSA_KVH),
            pl.BlockSpec((None, tq, LANES), lambda b, g, i: (g, b * nq + i, 0)),
        ],
        out_specs=pl.BlockSpec((tq, gw), lambda b, g, i: (b * nq + i, g)),
        out_shape=jax.ShapeDtypeStruct((m, NSA_HEADS * NSA_DH), BF16),
        compiler_params=_cparams("parallel", "parallel", "arbitrary"),
        name="nsa_attn_prompt",
    )(q, ckv, ckv, slc3, slc3, win3, win3, gates_g)


def _nsa_sample_kernel(pt_ref, qg_ref, qn_ref, ck_ref, cv_ref, page_ref, new_ref, win_ref, gate_ref,
                       o_ref, sel_scr, m_scr, l_scr, acc_scr, ocmp_scr,
                       *, dseq, nsel, pos0, n_pages, n_new, n_win):
    del pt_ref
    G, rep, dh = NSA_KVH, NSA_REP, NSA_DH
    rows = G * dseq * rep
    rpg = dseq * rep
    p_id = pl.program_id(1)
    row = lax.broadcasted_iota(jnp.int32, (rows, 1), 0)
    pos_rows = pos0 + ((row >> 3) & (dseq - 1))
    nbl = sel_scr.shape[1]
    groups = [(g * rpg, (g + 1) * rpg) for g in range(G)]

    @pl.when(p_id == 0)
    def _():
        s = jnp.concatenate([_dot_nt(qg_ref[g], ck_ref[g].astype(BF16)) for g in range(G)], axis=0)
        p, o_cmp = _cmp_branch(s, [cv_ref[g].astype(BF16) for g in range(G)], groups, pos_rows, nbl)
        ocmp_scr[...] = o_cmp
        imp = jnp.sum(p.reshape(rows // rep, rep, nbl), axis=1, keepdims=True)
        imp = jnp.broadcast_to(imp, (rows // rep, rep, nbl)).reshape(rows, nbl)
        sel_scr[...] = _select_blocks(imp, pos_rows, nsel)
        m_scr[...] = jnp.full(m_scr.shape, NEG_INF, F32)
        l_scr[...] = jnp.zeros(l_scr.shape, F32)
        acc_scr[...] = jnp.zeros(acc_scr.shape, F32)

    qn = qn_ref[...]

    def attend(k_all, v_all, bias):
        s = _dot_nt(qn, k_all) + bias
        m_old = m_scr[...]
        m_new = jnp.maximum(m_old, jnp.max(s, axis=1, keepdims=True))
        alpha = jnp.exp(m_old - m_new)
        pe = jnp.where(bias == 0.0, jnp.exp(s - m_new), 0.0)
        l_scr[...] = alpha * l_scr[...] + jnp.sum(pe, axis=1, keepdims=True)
        acc_scr[...] = alpha * acc_scr[...] + _dot(pe.astype(BF16), v_all)
        m_scr[...] = m_new

    page = page_ref[...]
    k0 = p_id * PAGE_SIZE
    key = k0 + lax.broadcasted_iota(jnp.int32, (1, PAGE_SIZE), 1)
    selexp = _expand_sel(sel_scr[...].astype(BF16), k0, PAGE_SIZE, nbl)
    bias = jnp.where(jnp.where(key <= pos_rows, selexp, 0.0) > 0.5, 0.0, NEG_INF)
    attend(page[:, :NSA_KV_W].astype(BF16), page[:, NSA_KV_W:].astype(BF16), bias)

    @pl.when(p_id == n_pages - 1)
    def _():
        past = n_pages * PAGE_SIZE
        new = new_ref[...]
        nk = new.shape[0]
        idx = lax.broadcasted_iota(jnp.int32, (1, nk), 1)
        key_n = past + idx
        selexp_n = _expand_sel(sel_scr[...].astype(BF16), past, nk, nbl)
        ok = jnp.where(idx < n_new, jnp.where(key_n <= pos_rows, selexp_n, 0.0), 0.0)
        attend(new[:, :NSA_KV_W].astype(BF16), new[:, NSA_KV_W:].astype(BF16),
               jnp.where(ok > 0.5, 0.0, NEG_INF))
        acc = acc_scr[...] / l_scr[...]
        o_slc = jnp.concatenate([acc[r0:r1, g * dh:(g + 1) * dh] for g, (r0, r1) in enumerate(groups)],
                                axis=0)
        win = win_ref[...]
        nw = win.shape[0]
        widx = lax.broadcasted_iota(jnp.int32, (1, nw), 1)
        kpos = pos0 - (n_win - n_new) + widx
        okw = jnp.where(widx < n_win, 1.0, 0.0)
        okw = jnp.where(kpos >= 0, okw, 0.0)
        okw = jnp.where(kpos <= pos_rows, okw, 0.0)
        okw = jnp.where(kpos > pos_rows - WINDOW, okw, 0.0)
        sw = _dot_nt(qn, win[:, :NSA_KV_W].astype(BF16)) + jnp.where(okw > 0.5, 0.0, NEG_INF)
        pw = jnp.exp(sw - jnp.max(sw, axis=1, keepdims=True))
        accw = _dot(pw.astype(BF16), win[:, NSA_KV_W:].astype(BF16)) / jnp.sum(pw, axis=1, keepdims=True)
        o_win = jnp.concatenate([accw[r0:r1, g * dh:(g + 1) * dh] for g, (r0, r1) in enumerate(groups)],
                                axis=0)
        gt = jax.nn.sigmoid(gate_ref[...])
        o_ref[...] = gt[:, 0:1] * ocmp_scr[...] + gt[:, 1:2] * o_slc + gt[:, 2:3] * o_win


def nsa_attn_sample(page_table, qg, qn, ckv, cache_slc, new_rows, win_rows, gates_s,
                    *, dseq, nsel, pos0, n_new, n_win):
    bt, n_pages = page_table.shape
    rows = NSA_KVH * dseq * NSA_REP
    nbl = ckv.shape[3]
    assert dseq & (dseq - 1) == 0 and NSA_REP == 8
    return pl.pallas_call(
        functools.partial(_nsa_sample_kernel, dseq=dseq, nsel=nsel, pos0=pos0, n_pages=n_pages,
                          n_new=n_new, n_win=n_win),
        grid_spec=pltpu.PrefetchScalarGridSpec(
            num_scalar_prefetch=1,
            grid=(bt, n_pages),
            in_specs=[
                pl.BlockSpec((None, NSA_KVH, dseq * NSA_REP, NSA_DH), lambda b, p, pt: (b, 0, 0, 0)),
                pl.BlockSpec((None, rows, NSA_KV_W), lambda b, p, pt: (b, 0, 0)),
                pl.BlockSpec((None, NSA_KVH, None, nbl, NSA_DH), lambda b, p, pt: (0, 0, b, 0, 0)),
                pl.BlockSpec((None, NSA_KVH, None, nbl, NSA_DH), lambda b, p, pt: (1, 0, b, 0, 0)),
                pl.BlockSpec((None, PAGE_SIZE, NSA_ROW_W), lambda b, p, pt: (pt[b, p], 0, 0)),
                pl.BlockSpec((None,) + new_rows.shape[1:], lambda b, p, pt: (b, 0, 0)),
                pl.BlockSpec((None,) + win_rows.shape[1:], lambda b, p, pt: (b, 0, 0)),
                pl.BlockSpec((None, rows, LANES), lambda b, p, pt: (b, 0, 0)),
            ],
            out_specs=pl.BlockSpec((None, rows, NSA_DH), lambda b, p, pt: (b, 0, 0)),
            scratch_shapes=[pltpu.VMEM((rows, nbl), F32), pltpu.VMEM((rows, 1), F32),
                            pltpu.VMEM((rows, 1), F32), pltpu.VMEM((rows, NSA_KV_W), F32),
                            pltpu.VMEM((rows, NSA_DH), F32)],
        ),
        out_shape=jax.ShapeDtypeStruct((bt, rows, NSA_DH), F32),
        compiler_params=_cparams("parallel", "arbitrary"),
        name="nsa_attn_sample",
    )(page_table, qg, qn, ckv, ckv, cache_slc, new_rows, win_rows, gates_s)


def _round_up(x, mult):
    return -(-x // mult) * mult


def nsa_mixer(xb, w, *, bt, pos0, page_table=None, cache_cmp=None, cache_slc=None, cache_win=None):
    m = xb.shape[0]
    L = m // bt
    G, R, dh = NSA_KVH, NSA_REP, NSA_DH
    proj = mm(xb, w['in_main'])
    gates = mm(xb, w['in_gate'], bn=LANES)[:, :3 * NSA_HEADS]
    cos, sin = _rope_tables(pos0 + jnp.arange(L), dh)
    if L % 8:
        cos, sin = jnp.tile(cos, (bt, 1)), jnp.tile(sin, (bt, 1))
    q, new_cmp, new_slc, new_win = nsa_prep(proj, cos, sin, rows_per_table=cos.shape[0])
    past = 0 if page_table is None else page_table.shape[1] * PAGE_SIZE
    total = past + L
    tp = _round_up(total, SEL_BLOCK)
    nsel = tp // SEL_BLOCK
    nbl = _round_up(tp // CMP_BLOCK, LANES)

    if page_table is None:
        src = new_cmp.reshape(m // PAGE_SIZE, PAGE_SIZE, NSA_ROW_W)
        pt = jnp.arange(m // PAGE_SIZE, dtype=jnp.int32).reshape(bt, L // PAGE_SIZE)
    else:
        src = cache_cmp.reshape(-1, PAGE_SIZE, NSA_ROW_W)
        pt = page_table
    xc = cmp_relayout(src, pt)
    nb = xc.shape[2] // bt
    ccos, csin = _rope_tables((jnp.arange(nb) + 1) * CMP_BLOCK - 1, dh)
    ccos = jnp.stack([jnp.tile(ccos, (bt, 1)), jnp.ones((bt * nb, dh), F32)])
    csin = jnp.stack([jnp.tile(csin, (bt, 1)), jnp.zeros((bt * nb, dh), F32)])
    ckv = compress(xc, w['pe'], w['c1'], w['c2'], ccos, csin).reshape(2, G, bt, nb, dh)
    ckv = jnp.pad(ckv, ((0, 0), (0, 0), (0, 0), (0, nbl - nb), (0, 0)))

    g5 = gates.reshape(bt, L, G, R, 3)
    if page_table is None:
        gates_g = jnp.transpose(g5, (2, 0, 1, 3, 4)).reshape(G, m, R * 3)
        gates_g = jnp.pad(gates_g, ((0, 0), (0, 0), (0, LANES - R * 3)))
        o = nsa_attn_prompt(q, ckv, new_slc, new_win, gates_g, bt=bt, nsel=nsel)
        win_keep = new_win.reshape(bt, L, NSA_ROW_W)[:, L - min(WINDOW, L):]
    else:
        q5 = q.reshape(bt, L, G, R, dh)
        qg = jnp.transpose(q5, (0, 2, 1, 3, 4)).reshape(bt, G, L * R, dh)
        qn = jnp.einsum('bgcd,gh->bgchd', qg, jnp.eye(G, dtype=BF16)).reshape(bt, G * L * R, G * dh)
        gates_s = jnp.transpose(g5, (0, 2, 1, 3, 4)).reshape(bt, G * L * R, 3)
        gates_s = jnp.pad(gates_s, ((0, 0), (0, 0), (0, LANES - 3)))
        new3 = new_slc.reshape(bt, L, NSA_ROW_W)
        new_pad = jnp.pad(new3, ((0, 0), (0, _round_up(L, LANES) - L), (0, 0)))
        prefix = cache_win.reshape(bt, -1, NSA_ROW_W)
        win_all = jnp.concatenate([prefix, new_win.reshape(bt, L, NSA_ROW_W)], axis=1)
        n_win = win_all.shape[1]
        win_pad = jnp.pad(win_all, ((0, 0), (0, _round_up(n_win, LANES) - n_win), (0, 0)))
        o = nsa_attn_sample(pt, qg, qn, ckv, cache_slc.reshape(-1, PAGE_SIZE, NSA_ROW_W), new_pad, win_pad,
                            gates_s, dseq=L, nsel=nsel, pos0=pos0, n_new=L, n_win=n_win)
        o = jnp.transpose(o.reshape(bt, G, L, R, dh), (0, 2, 1, 3, 4)).reshape(m, NSA_HEADS * dh).astype(BF16)
        win_keep = win_all[:, n_win - prefix.shape[1]:]
    shape5 = (bt, L, 2, G, dh)
    return o, new_cmp.reshape(shape5), new_slc.reshape(shape5), win_keep.reshape(bt, -1, 2, G, dh)


def kernel(x_prompt, x_sample, state_ssm, state_ret, cache_cmp_kv, cache_slc_kv, cache_win_kv, state_pool, page_table, ln_g, ln_b, ffn_w_gate, ffn_w_up, ffn_w_down, ssm_a_re, ssm_a_im, ssm_log_dt, ssm_b_re, ssm_b_im, ssm_c_re, ssm_c_im, ssm_d, ssm_w_glu, ret_w_in, ret_norm_g, ret_w_out, nsa_w_in, nsa_pe_ck, nsa_w_ck1, nsa_w_ck2, nsa_pe_cv, nsa_w_cv1, nsa_w_cv2, nsa_w_out, pool_w, pool_scale):
    past_len = page_table.shape[1] * PAGE_SIZE

    wg = ffn_w_gate.astype(BF16)
    wu = ffn_w_up.astype(BF16)
    wd = ffn_w_down.astype(BF16)
    w_glu = ssm_w_glu[0].astype(BF16)
    r_w_in = ret_w_in[0].astype(BF16)
    r_w_out = ret_w_out[0].astype(BF16)
    n_main = NSA_HEADS * NSA_DH + 6 * NSA_KV_W
    nsa_w = {
        'in_main': nsa_w_in[0, :, :n_main].astype(BF16),
        'in_gate': jnp.pad(nsa_w_in[0, :, n_main:], ((0, 0), (0, LANES - 3 * NSA_HEADS))).astype(BF16),
        'pe': jnp.stack([nsa_pe_ck[0], nsa_pe_cv[0]]).astype(F32).reshape(2, 1, CMP_BLOCK * NSA_DH),
        'c1': jnp.stack([nsa_w_ck1[0], nsa_w_cv1[0]]).astype(BF16),
        'c2': jnp.stack([nsa_w_ck2[0], nsa_w_cv2[0]]).astype(BF16),
    }
    n_w_out = nsa_w_out[0].astype(BF16)
    p_w = pool_w[0].astype(BF16)
    a_tab, bblk, cblk = _ssm_tables(ssm_a_re[0], ssm_a_im[0], ssm_log_dt[0], ssm_b_re[0], ssm_b_im[0],
                                    ssm_c_re[0], ssm_c_im[0])

    def ffn(xf, xb, i, s, ln_idx, **kw):
        h = dual_mm(xb, wg, wu, combine=_swiglu_combine, n=D_FF, widx_a=(i, s), widx_b=(i, s))
        y = mm(h, wd, widx=(i, s))
        return res_ln(y, xf, ln_g[i, ln_idx], ln_b[i, ln_idx], scale=0.5, **kw)

    def stream(x, pos0, is_prompt):
        bt, L, _ = x.shape
        m = bt * L
        xf = x.astype(F32).reshape(m, D_MODEL)
        xb = xf.astype(BF16)
        outs = {}
        for i in range(DEPTH):
            mixer = i % N_MIXERS
            lnm = dict(scale=1.0)
            if mixer == 0:
                big = L % 8 == 0
                if big:
                    u_tm, _ = ffn(xf, xb, i, 0, 0, bt=bt, out_tm=True)
                else:
                    xf, xb = ffn(xf, xb, i, 0, 0)
                    u_tm = jnp.transpose(xf.reshape(bt, L, D_MODEL), (1, 0, 2)).reshape(m, D_MODEL)
                h0 = jnp.zeros((bt, SSM_GROUPS, SSM_N, 2), F32) if is_prompt else state_ssm[0]
                yg, outs['ssm'] = ssm_scan(u_tm, h0, a_tab, bblk, cblk, ssm_d[0], bt=bt)
                y = dual_mm(yg, w_glu, w_glu, combine=_glu_combine, n=D_MODEL, col0_b=D_MODEL,
                            out_dtype=F32)
                if big:
                    xf, xb = res_ln(y, u_tm, ln_g[i, 1], ln_b[i, 1], bt=bt, in_tm=True, **lnm)
                else:
                    y = jnp.transpose(y.reshape(L, bt, D_MODEL), (1, 0, 2)).reshape(m, D_MODEL)
                    xf, xb = res_ln(y, xf, ln_g[i, 1], ln_b[i, 1], **lnm)
            else:
                xf, xb = ffn(xf, xb, i, 0, 0)
                if mixer == 1:
                    proj = mm(xb, r_w_in)
                    s0 = None if is_prompt else state_ret[0]
                    o, outs['ret'] = retention_core(proj, s0, ret_norm_g[0], bt=bt, pos0=pos0)
                    y = mm(o, r_w_out)
                elif mixer == 2:
                    if is_prompt:
                        o, c_new, s_new, w_new = nsa_mixer(xb, nsa_w, bt=bt, pos0=pos0)
                    else:
                        o, c_new, s_new, w_new = nsa_mixer(
                            xb, nsa_w, bt=bt, pos0=pos0, page_table=page_table,
                            cache_cmp=cache_cmp_kv[0], cache_slc=cache_slc_kv[0], cache_win=cache_win_kv[0])
                    outs['cmp'], outs['slc'], outs['win'] = c_new, s_new, w_new
                    y = mm(o, n_w_out)
                else:
                    prefix = jnp.zeros((bt, POOL_PAST, D_MODEL), F32) if is_prompt else state_pool[0]
                    y = pool_core(xf, prefix, p_w, pool_scale[0], bt=bt, pos0=pos0)
                    x3 = xf.reshape(bt, L, D_MODEL)
                    if L < POOL_PAST:
                        x3 = jnp.concatenate([prefix.astype(F32)[:, L:], x3], axis=1)
                    outs['pool'] = x3[:, -POOL_PAST:]
                xf, xb = res_ln(y, xf, ln_g[i, 1], ln_b[i, 1], **lnm)
            xf, xb = ffn(xf, xb, i, 1, 2)
        return xf.reshape(bt, L, D_MODEL), outs

    yp, op = stream(x_prompt, 0, True)
    ys, os_ = stream(x_sample, past_len, False)
    return (yp.astype(x_prompt.dtype), ys.astype(x_sample.dtype),
            op['ssm'][None], os_['ssm'][None], op['ret'][None], os_['ret'][None],
            op['cmp'][None], os_['cmp'][None], op['slc'][None], os_['slc'][None],
            op['win'][None], os_['win'][None], op['pool'][None], os_['pool'][None])
```

```python
import functools
import math

import jax
import jax.numpy as jnp
from jax import lax
from jax.experimental import pallas as pl
from jax.experimental.pallas import tpu as pltpu

F32 = jnp.float32
BF16 = jnp.bfloat16

D_MODEL = 4096
DEPTH = 4
PAGE_SIZE = 128
N_MIXERS = 4
D_FF = 2 * D_MODEL
DN_ALPHA = (2.0 * DEPTH) ** 0.25
LN_EPS = 1e-5
ROPE_THETA = 10000.0
NEG_INF = -1e30

SSM_GROUP = 16
SSM_GROUPS = D_MODEL // SSM_GROUP
SSM_N = 64
SSM_TILE_GROUPS = 16
SSM_TILE_CH = SSM_TILE_GROUPS * SSM_GROUP
SSM_TILE_ST = SSM_TILE_GROUPS * SSM_N

RET_DK = 256
RET_HEADS = D_MODEL // RET_DK
RET_DV = 2 * RET_DK
RET_CHUNK = 128

NSA_DH = 128
NSA_HEADS = D_MODEL // NSA_DH
NSA_KVH = 4
NSA_REP = NSA_HEADS // NSA_KVH
CMP_BLOCK = 32
SEL_BLOCK = 64
N_SEL = 16
WINDOW = 512
SEL_FORCE = 1e4
NSA_KV_W = NSA_KVH * NSA_DH
NSA_ROW_W = 2 * NSA_KV_W
NSA_TQ = 128

POOL_WINDOWS = (2, 4, 8, 16)
POOL_CH = D_MODEL // len(POOL_WINDOWS)
POOL_PAST = max(POOL_WINDOWS) - 1
POOL_HALO = 16

LANES = 128
V7X_VMEM_BYTES = 64 * 1024 * 1024
VMEM_LIMIT = V7X_VMEM_BYTES - 8 * 1024 * 1024


def _cparams(*sem):
    return pltpu.CompilerParams(dimension_semantics=sem, vmem_limit_bytes=VMEM_LIMIT)


def _dot(a, b):
    return jnp.dot(a, b, preferred_element_type=F32)


def _dot_nt(a, b):
    return lax.dot_general(a, b, (((1,), (1,)), ((), ())), preferred_element_type=F32)


def _dot_tn(a, b):
    return lax.dot_general(a, b, (((0,), (0,)), ((), ())), preferred_element_type=F32)


def _mm_kernel(x_ref, w_ref, o_ref):
    o_ref[...] = _dot(x_ref[...], w_ref[...].astype(BF16)).astype(o_ref.dtype)


def _w_spec(k, bn, col0_blocks, widx):
    lead = tuple(widx)
    shape = (None,) * len(lead) + (k, bn)
    return pl.BlockSpec(shape, lambda i, j: lead + (0, j + col0_blocks))


def mm(x, w, *, widx=(), n=None, col0=0, out_dtype=F32, bm=None, bn=512):
    m, k = x.shape
    n = w.shape[-1] if n is None else n
    if bm is None:
        bm = min(m, 1024 if k <= 4096 else 512)
    bn = min(bn, n)
    assert m % bm == 0 and n % bn == 0 and col0 % bn == 0
    return pl.pallas_call(
        _mm_kernel,
        grid=(m // bm, n // bn),
        in_specs=[pl.BlockSpec((bm, k), lambda i, j: (i, 0)),
                  _w_spec(k, bn, col0 // bn, widx)],
        out_specs=pl.BlockSpec((bm, bn), lambda i, j: (i, j)),
        out_shape=jax.ShapeDtypeStruct((m, n), out_dtype),
        compiler_params=_cparams("parallel", "arbitrary"),
        name="mm",
    )(x, w)


def _dual_mm_kernel(x_ref, wa_ref, wb_ref, o_ref, *, combine):
    x = x_ref[...]
    a = _dot(x, wa_ref[...].astype(BF16))
    b = _dot(x, wb_ref[...].astype(BF16))
    o_ref[...] = combine(a, b).astype(o_ref.dtype)


def _swiglu_combine(a, b):
    return a * jax.nn.sigmoid(a) * b


def _glu_combine(a, b):
    return a * jax.nn.sigmoid(b)


def dual_mm(x, wa, wb, *, combine, n, widx_a=(), widx_b=(), col0_a=0, col0_b=0,
            out_dtype=BF16, bm=None, bn=512):
    m, k = x.shape
    if bm is None:
        bm = min(m, 1024)
    assert m % bm == 0 and n % bn == 0 and col0_a % bn == 0 and col0_b % bn == 0
    return pl.pallas_call(
        functools.partial(_dual_mm_kernel, combine=combine),
        grid=(m // bm, n // bn),
        in_specs=[pl.BlockSpec((bm, k), lambda i, j: (i, 0)),
                  _w_spec(k, bn, col0_a // bn, widx_a),
                  _w_spec(k, bn, col0_b // bn, widx_b)],
        out_specs=pl.BlockSpec((bm, bn), lambda i, j: (i, j)),
        out_shape=jax.ShapeDtypeStruct((m, n), out_dtype),
        compiler_params=_cparams("parallel", "arbitrary"),
        name="dual_mm",
    )(x, wa, wb)


def _res_ln_kernel(y_ref, r_ref, g_ref, b_ref, of_ref, ob_ref, *, scale):
    z = DN_ALPHA * r_ref[...] + scale * y_ref[...].astype(F32)
    mu = jnp.mean(z, axis=-1, keepdims=True)
    zc = z - mu
    var = jnp.mean(zc * zc, axis=-1, keepdims=True)
    out = zc * lax.rsqrt(var + LN_EPS) * g_ref[...] + b_ref[...]
    of_ref[...] = out
    ob_ref[...] = out.astype(BF16)


def res_ln(y, resid, g, b, *, scale=1.0, bm=256):
    m, d = resid.shape
    bm = min(bm, m)
    assert m % bm == 0
    row = pl.BlockSpec((bm, d), lambda i: (i, 0))
    vec = pl.BlockSpec((1, d), lambda i: (0, 0))
    return pl.pallas_call(
        functools.partial(_res_ln_kernel, scale=scale),
        grid=(m // bm,),
        in_specs=[row, row, vec, vec],
        out_specs=[row, row],
        out_shape=[jax.ShapeDtypeStruct((m, d), F32), jax.ShapeDtypeStruct((m, d), BF16)],
        compiler_params=_cparams("parallel"),
        name="res_ln",
    )(y, resid, g.reshape(1, d), b.reshape(1, d))


MM_LN_COLS = 1024
MM_LN_ROWS = 64


def _mm_res_ln_kernel(h_ref, w_ref, r_ref, g_ref, b_ref, of_ref, ob_ref, *, scale, nk):
    k = pl.program_id(1)
    bm, d = of_ref.shape
    h = h_ref[...]
    for n0 in range(0, d, MM_LN_COLS):
        cols = slice(n0, n0 + MM_LN_COLS)
        part = _dot(h, w_ref[:, cols])

        @pl.when(k == 0)
        def _():
            of_ref[:, cols] = part

        @pl.when(k > 0)
        def _():
            of_ref[:, cols] += part

    @pl.when(k == nk - 1)
    def _():
        g = g_ref[...]
        b = b_ref[...]

        def norm_rows(i, carry):
            rows = pl.ds(pl.multiple_of(i * MM_LN_ROWS, MM_LN_ROWS), MM_LN_ROWS)
            z = DN_ALPHA * r_ref[rows, :] + scale * of_ref[rows, :]
            mu = jnp.mean(z, axis=-1, keepdims=True)
            zc = z - mu
            var = jnp.mean(zc * zc, axis=-1, keepdims=True)
            out = zc * lax.rsqrt(var + LN_EPS) * g + b
            of_ref[rows, :] = out
            ob_ref[rows, :] = out.astype(BF16)
            return carry

        lax.fori_loop(0, bm // MM_LN_ROWS, norm_rows, 0)


def mm_res_ln(h, w, resid, g, b, *, widx=(), scale=1.0, bm=512, bk=1024):
    m, kdim = h.shape
    d = resid.shape[1]
    if m < bm or m % bm:
        return res_ln(mm(h, w, widx=widx), resid, g, b, scale=scale)
    assert kdim % bk == 0 and d % MM_LN_COLS == 0 and bm % MM_LN_ROWS == 0
    nk = kdim // bk
    lead = tuple(widx)
    row = pl.BlockSpec((bm, d), lambda i, k: (i, 0))
    vec = pl.BlockSpec((1, d), lambda i, k: (0, 0))
    return pl.pallas_call(
        functools.partial(_mm_res_ln_kernel, scale=scale, nk=nk),
        grid=(m // bm, nk),
        in_specs=[pl.BlockSpec((bm, bk), lambda i, k: (i, k)),
                  pl.BlockSpec((None,) * len(lead) + (bk, d), lambda i, k: lead + (k, 0)),
                  pl.BlockSpec((bm, d), lambda i, k: (i, 0), pipeline_mode=pl.Buffered(1)),
                  vec, vec],
        out_specs=[row, row],
        out_shape=[jax.ShapeDtypeStruct((m, d), F32), jax.ShapeDtypeStruct((m, d), BF16)],
        compiler_params=_cparams("parallel", "arbitrary"),
        name="mm_res_ln",
    )(h, w, resid, g.reshape(1, d), b.reshape(1, d))


def _ssm_kernel(u_ref, h0_ref, a_ref, bblk_ref, cblk_ref, d_ref, y_ref, hout_ref,
                h_scr, u_scr, y_scr, bu_scr, hs_scr, *, bt, tc, nchunks):
    c = pl.program_id(1)
    st = SSM_TILE_ST
    nl = SSM_TILE_CH // LANES

    @pl.when(c == 0)
    def _():
        h_scr[...] = jnp.zeros_like(h_scr)
        h_scr[0:bt, :] = h0_ref[...]

    for b in range(bt):
        for j in range(nl):
            u_scr[j, pl.ds(b, tc, stride=bt), :] = u_ref[b, :, j * LANES:(j + 1) * LANES]
    u = jnp.concatenate([u_scr[j] for j in range(nl)], axis=1)
    bu_scr[...] = _dot(u.astype(BF16), bblk_ref[...])
    ar = a_ref[0:1, :]
    ai = a_ref[1:2, :]

    def advance(hr, hi, sr, si):
        return ar * hr - ai * hi + sr, ar * hi + ai * hr + si

    if bt == 8:
        def step(t, carry):
            r0 = pl.multiple_of(t * 8, 8)
            hr, hi = advance(*carry, bu_scr[pl.ds(r0, 8), 0:st], bu_scr[pl.ds(r0, 8), st:2 * st])
            hs_scr[pl.ds(r0, 8), 0:st] = hr
            hs_scr[pl.ds(r0, 8), st:2 * st] = hi
            return hr, hi
        n_it = tc
    else:
        low = lax.broadcasted_iota(jnp.int32, (8, st), 0) < bt

        def step(t2, carry):
            r0 = pl.multiple_of(t2 * 8, 8)
            sr = bu_scr[pl.ds(r0, 8), 0:st]
            si = bu_scr[pl.ds(r0, 8), st:2 * st]
            e_r, e_i = advance(*carry, sr, si)
            o_r, o_i = advance(pltpu.roll(e_r, bt, 0), pltpu.roll(e_i, bt, 0), sr, si)
            hs_scr[pl.ds(r0, 8), 0:st] = jnp.where(low, e_r, o_r)
            hs_scr[pl.ds(r0, 8), st:2 * st] = jnp.where(low, e_i, o_i)
            return pltpu.roll(o_r, bt, 0), pltpu.roll(o_i, bt, 0)
        n_it = tc // 2

    hr, hi = lax.fori_loop(0, n_it, step, (h_scr[:, 0:st], h_scr[:, st:2 * st]))
    h_scr[:, 0:st] = hr
    h_scr[:, st:2 * st] = hi
    y = jax.nn.gelu(_dot(hs_scr[...].astype(BF16), cblk_ref[...]) + u * d_ref[...])
    for j in range(nl):
        y_scr[j] = y[:, j * LANES:(j + 1) * LANES]
    for b in range(bt):
        for j in range(nl):
            y_ref[b, :, j * LANES:(j + 1) * LANES] = y_scr[j, pl.ds(b, tc, stride=bt), :].astype(y_ref.dtype)

    @pl.when(c == nchunks - 1)
    def _():
        hout_ref[...] = h_scr[0:bt, :]


def _ssm_tables(a_re, a_im, log_dt, b_re, b_im, c_re, c_im):
    ar = a_re.astype(F32)
    ai = a_im.astype(F32)
    dt = jnp.exp(log_dt.astype(F32))[:, None]
    mag = jnp.exp(ar * dt)
    abar_r = mag * jnp.cos(ai * dt)
    abar_i = mag * jnp.sin(ai * dt)
    den = ar * ar + ai * ai
    zr = abar_r - 1.0
    coef_r = (zr * ar + abar_i * ai) / den
    coef_i = (abar_i * ar - zr * ai) / den
    br = b_re.astype(F32)
    bi = b_im.astype(F32)
    bbar_r = coef_r[..., None] * br - coef_i[..., None] * bi
    bbar_i = coef_r[..., None] * bi + coef_i[..., None] * br
    gt = SSM_TILE_GROUPS
    nt = SSM_GROUPS // gt
    eye = jnp.eye(gt, dtype=F32)
    bb = jnp.stack([bbar_r, bbar_i]).reshape(2, nt, gt, SSM_N, SSM_GROUP)
    bblk = jnp.einsum('rtgnc,gh->tgcrhn', bb, eye).reshape(nt, SSM_TILE_CH, 2 * SSM_TILE_ST)
    cc = jnp.stack([c_re.astype(F32), -c_im.astype(F32)]).reshape(2, nt, gt, SSM_GROUP, SSM_N)
    cblk = jnp.einsum('rtgcn,gh->trgnhc', cc, eye).reshape(nt, 2 * SSM_TILE_ST, SSM_TILE_CH)
    a_tab = jnp.stack([abar_r.reshape(nt, SSM_TILE_ST), abar_i.reshape(nt, SSM_TILE_ST)], axis=1)
    return a_tab, bblk.astype(BF16), cblk.astype(BF16)


def ssm_scan(x, h0, a_tab, bblk, cblk, d_skip, *, bt):
    m, d = x.shape
    L = m // bt
    assert bt in (4, 8)
    nt = SSM_GROUPS // SSM_TILE_GROUPS
    tc = min(L, 128)
    assert L % tc == 0 and (tc * bt) % 16 == 0
    nchunks = L // tc
    rows = tc * bt
    h0t = h0.astype(F32).reshape(bt, nt, SSM_TILE_GROUPS, SSM_N, 2)
    h0t = jnp.transpose(h0t, (1, 0, 4, 2, 3)).reshape(nt, bt, 2 * SSM_TILE_ST)
    tok = pl.BlockSpec((bt, tc, SSM_TILE_CH), lambda j, c: (0, c, j))
    y, hout = pl.pallas_call(
        functools.partial(_ssm_kernel, bt=bt, tc=tc, nchunks=nchunks),
        grid=(nt, nchunks),
        in_specs=[
            tok,
            pl.BlockSpec((None, bt, 2 * SSM_TILE_ST), lambda j, c: (j, 0, 0)),
            pl.BlockSpec((None, 2, SSM_TILE_ST), lambda j, c: (j, 0, 0)),
            pl.BlockSpec((None, SSM_TILE_CH, 2 * SSM_TILE_ST), lambda j, c: (j, 0, 0)),
            pl.BlockSpec((None, 2 * SSM_TILE_ST, SSM_TILE_CH), lambda j, c: (j, 0, 0)),
            pl.BlockSpec((1, SSM_TILE_CH), lambda j, c: (0, j)),
        ],
        out_specs=[tok, pl.BlockSpec((None, bt, 2 * SSM_TILE_ST), lambda j, c: (j, 0, 0))],
        out_shape=[jax.ShapeDtypeStruct((bt, L, d), BF16),
                   jax.ShapeDtypeStruct((nt, bt, 2 * SSM_TILE_ST), F32)],
        scratch_shapes=[pltpu.VMEM((8, 2 * SSM_TILE_ST), F32),
                        pltpu.VMEM((SSM_TILE_CH // LANES, rows, LANES), F32),
                        pltpu.VMEM((SSM_TILE_CH // LANES, rows, LANES), F32),
                        pltpu.VMEM((rows, 2 * SSM_TILE_ST), F32),
                        pltpu.VMEM((rows, 2 * SSM_TILE_ST), F32)],
        compiler_params=_cparams("parallel", "arbitrary"),
        name="ssm_scan",
    )(x.reshape(bt, L, d), h0t, a_tab, bblk, cblk, d_skip.astype(F32).reshape(1, d))
    hfin = hout.reshape(nt, bt, 2, SSM_TILE_GROUPS, SSM_N)
    hfin = jnp.transpose(hfin, (1, 0, 3, 4, 2)).reshape(bt, SSM_GROUPS, SSM_N, 2)
    return y.reshape(m, d), hfin


def _ret_kernel(*refs, nc, zero_init):
    if zero_init:
        (q_ref, k_ref, v_ref, g_ref, cos_ref, sin_ref, intra_ref, qd_ref, kd_ref, cd_ref, ng_ref,
         o_ref, sfin_ref, s_scr) = refs
        s0_ref = None
    else:
        (q_ref, k_ref, v_ref, g_ref, cos_ref, sin_ref, intra_ref, qd_ref, kd_ref, cd_ref, ng_ref,
         s0_ref, o_ref, sfin_ref, s_scr) = refs
    c = pl.program_id(2)

    @pl.when(c == 0)
    def _():
        if zero_init:
            s_scr[...] = jnp.zeros_like(s_scr)
        else:
            s_scr[...] = s0_ref[...]

    cos = cos_ref[...]
    sin = sin_ref[...]
    half = RET_DK // 2

    def rope(x):
        x1 = x[:, :half]
        x2 = x[:, half:]
        return jnp.concatenate([x1 * cos - x2 * sin, x2 * cos + x1 * sin], axis=1)

    q = rope(q_ref[...])
    k = rope(k_ref[...]) * RET_DK ** -0.5
    qb = q.astype(BF16)
    vb = v_ref[...].astype(BF16)
    sc = _dot_nt(qb, k.astype(BF16)) * intra_ref[...]
    s_old = s_scr[...]
    o = _dot(sc.astype(BF16), vb) + _dot(qb, s_old.astype(BF16)) * qd_ref[...]
    s_new = s_old * cd_ref[...] + _dot_tn((k * kd_ref[...]).astype(BF16), vb)
    s_scr[...] = s_new
    mu = jnp.mean(o, axis=-1, keepdims=True)
    oc = o - mu
    var = jnp.mean(oc * oc, axis=-1, keepdims=True)
    on = oc * lax.rsqrt(var + LN_EPS) * ng_ref[...]
    g = g_ref[...]
    o_ref[...] = (g * jax.nn.sigmoid(g) * on).astype(o_ref.dtype)

    @pl.when(c == nc - 1)
    def _():
        sfin_ref[...] = s_new


def retention_core(proj, s0, norm_g, *, bt, pos0):
    m, n = proj.shape
    L = m // bt
    H = RET_HEADS
    C = math.gcd(L, RET_CHUNK)
    nc = L // C
    pos = (pos0 + jnp.arange(L)).astype(F32)
    inv = 1.0 / (ROPE_THETA ** jnp.linspace(0.0, 1.0, RET_DK // 2, dtype=F32))
    ang = pos[:, None] * inv[None, :]
    cos_t, sin_t = jnp.cos(ang), jnp.sin(ang)
    log_g = jnp.log1p(-jnp.exp2(-5.0 - jnp.arange(H, dtype=F32)))
    idx = jnp.arange(C, dtype=F32)
    diff = idx[:, None] - idx[None, :]
    intra = jnp.where(diff >= 0, jnp.exp(log_g[:, None, None] * jnp.maximum(diff, 0.0)), 0.0)
    q_dec = jnp.exp(log_g[:, None] * (idx[None, :] + 1.0))[..., None]
    k_dec = jnp.exp(log_g[:, None] * (C - 1.0 - idx[None, :]))[..., None]
    chunk_dec = jnp.exp(log_g * C).reshape(H, 1, 1)
    zero_init = s0 is None
    proj3 = proj.reshape(bt, L, n)
    kb0 = H
    vb0 = 2 * H * RET_DK // RET_DV
    gb0 = vb0 + H
    in_specs = [
        pl.BlockSpec((None, C, RET_DK), lambda b, h, c: (b, c, h)),
        pl.BlockSpec((None, C, RET_DK), lambda b, h, c: (b, c, kb0 + h)),
        pl.BlockSpec((None, C, RET_DV), lambda b, h, c: (b, c, vb0 + h)),
        pl.BlockSpec((None, C, RET_DV), lambda b, h, c: (b, c, gb0 + h)),
        pl.BlockSpec((C, RET_DK // 2), lambda b, h, c: (c, 0)),
        pl.BlockSpec((C, RET_DK // 2), lambda b, h, c: (c, 0)),
        pl.BlockSpec((None, C, C), lambda b, h, c: (h, 0, 0)),
        pl.BlockSpec((None, C, 1), lambda b, h, c: (h, 0, 0)),
        pl.BlockSpec((None, C, 1), lambda b, h, c: (h, 0, 0)),
        pl.BlockSpec((None, 1, 1), lambda b, h, c: (h, 0, 0)),
        pl.BlockSpec((1, RET_DV), lambda b, h, c: (0, h)),
    ]
    args = [proj3, proj3, proj3, proj3, cos_t, sin_t, intra, q_dec, k_dec, chunk_dec,
            norm_g.astype(F32).reshape(1, H * RET_DV)]
    st_spec = pl.BlockSpec((None, None, RET_DK, RET_DV), lambda b, h, c: (b, h, 0, 0))
    if not zero_init:
        in_specs.append(st_spec)
        args.append(s0.astype(F32))
    o, sfin = pl.pallas_call(
        functools.partial(_ret_kernel, nc=nc, zero_init=zero_init),
        grid=(bt, H, nc),
        in_specs=in_specs,
        out_specs=[pl.BlockSpec((None, C, RET_DV), lambda b, h, c: (b, c, h)), st_spec],
        out_shape=[jax.ShapeDtypeStruct((bt, L, H * RET_DV), BF16),
                   jax.ShapeDtypeStruct((bt, H, RET_DK, RET_DV), F32)],
        scratch_shapes=[pltpu.VMEM((RET_DK, RET_DV), F32)],
        compiler_params=_cparams("parallel", "parallel", "arbitrary"),
        name="retention_core",
    )(*args)
    return o.reshape(m, H * RET_DV), sfin


def _pool_kernel(*refs, tile, pos0, has_halo):
    if has_halo:
        x_ref, halo_ref, pre_ref, w_ref, sc_ref, y_ref, xx_scr = refs
    else:
        x_ref, pre_ref, w_ref, sc_ref, y_ref, xx_scr = refs
        halo_ref = None
    i = pl.program_id(1)
    hl = POOL_HALO
    if has_halo:
        @pl.when(i == 0)
        def _():
            xx_scr[0:hl, :] = pre_ref[...]

        @pl.when(i > 0)
        def _():
            xx_scr[0:hl, :] = halo_ref[...]
    else:
        xx_scr[0:hl, :] = pre_ref[...]
    xx_scr[hl:hl + tile, :] = x_ref[...]
    pos = pos0 + i * tile + lax.broadcasted_iota(jnp.int32, (tile, 1), 0)
    for gi, w in enumerate(POOL_WINDOWS):
        c0, c1 = gi * POOL_CH, (gi + 1) * POOL_CH
        xs = xx_scr[hl:hl + tile, c0:c1]
        acc = xs
        for k in range(1, w):
            acc = acc + xx_scr[hl - k:hl - k + tile, c0:c1]
        cnt = jnp.minimum(w, pos + 1).astype(F32)
        dlt = (acc / cnt - xs).astype(BF16)
        y_ref[:, c0:c1] = _dot(dlt, w_ref[gi]) * sc_ref[:, c0:c1]


def pool_core(xf, prefix, w_pool, scale, *, bt, pos0):
    m, d = xf.shape
    L = m // bt
    tile = min(L, 256)
    assert L % tile == 0
    has_halo = L > tile
    x3 = xf.reshape(bt, L, d)
    pre = jnp.pad(prefix.astype(F32), ((0, 0), (POOL_HALO - POOL_PAST, 0), (0, 0)))
    hb = tile // POOL_HALO
    row = pl.BlockSpec((None, tile, d), lambda b, i: (b, i, 0))
    in_specs = [row]
    args = [x3]
    if has_halo:
        in_specs.append(pl.BlockSpec((None, POOL_HALO, d), lambda b, i: (b, jnp.maximum(i * hb - 1, 0), 0)))
        args.append(x3)
    in_specs += [
        pl.BlockSpec((None, POOL_HALO, d), lambda b, i: (b, 0, 0)),
        pl.BlockSpec((len(POOL_WINDOWS), POOL_CH, POOL_CH), lambda b, i: (0, 0, 0)),
        pl.BlockSpec((1, d), lambda b, i: (0, 0)),
    ]
    args += [pre, w_pool, scale.astype(F32).reshape(1, d)]
    y = pl.pallas_call(
        functools.partial(_pool_kernel, tile=tile, pos0=pos0, has_halo=has_halo),
        grid=(bt, L // tile),
        in_specs=in_specs,
        out_specs=row,
        out_shape=jax.ShapeDtypeStruct((bt, L, d), F32),
        scratch_shapes=[pltpu.VMEM((POOL_HALO + tile, d), F32)],
        compiler_params=_cparams("parallel", "arbitrary"),
        name="pool_core",
    )(*args)
    return y.reshape(m, d)


def _rope_tables(pos, dim):
    inv = 1.0 / (ROPE_THETA ** (jnp.arange(0, dim, 2, dtype=F32) / dim))
    ang = pos.astype(F32)[:, None] * inv[None, :]
    cos = jnp.cos(ang)
    sin = jnp.sin(ang)
    return jnp.concatenate([cos, cos], axis=1), jnp.concatenate([-sin, sin], axis=1)


def _rope128(x, cos, sin):
    return x * cos + pltpu.roll(x, NSA_DH // 2, 1) * sin


def _nsa_prep_kernel(p_ref, cos_ref, sin_ref, q_ref, cmp_ref, slc_ref, win_ref):
    cos = cos_ref[...]
    sin = sin_ref[...]
    dh = NSA_DH
    qw = NSA_HEADS * dh
    for h in range(NSA_HEADS):
        x = p_ref[:, h * dh:(h + 1) * dh]
        q_ref[:, h * dh:(h + 1) * dh] = (_rope128(x, cos, sin) * dh ** -0.5).astype(q_ref.dtype)
    kvw = NSA_KV_W
    cmp_ref[...] = p_ref[:, qw:qw + 2 * kvw]
    for dst, base in ((slc_ref, qw + 2 * kvw), (win_ref, qw + 4 * kvw)):
        for g in range(NSA_KVH):
            dst[:, g * dh:(g + 1) * dh] = _rope128(p_ref[:, base + g * dh:base + (g + 1) * dh], cos, sin)
        dst[:, kvw:2 * kvw] = p_ref[:, base + kvw:base + 2 * kvw]


def nsa_prep(proj, cos, sin, *, rows_per_table):
    m, n = proj.shape
    bm = min(m, 256)
    assert m % bm == 0 and rows_per_table % bm == 0
    nt = rows_per_table // bm
    row = lambda w: pl.BlockSpec((bm, w), lambda i: (i, 0))
    tab = pl.BlockSpec((bm, NSA_DH), lambda i: (i % nt, 0))
    return pl.pallas_call(
        _nsa_prep_kernel,
        grid=(m // bm,),
        in_specs=[row(n), tab, tab],
        out_specs=[row(NSA_HEADS * NSA_DH), row(NSA_ROW_W), row(NSA_ROW_W), row(NSA_ROW_W)],
        out_shape=[jax.ShapeDtypeStruct((m, NSA_HEADS * NSA_DH), BF16)]
        + [jax.ShapeDtypeStruct((m, NSA_ROW_W), F32)] * 3,
        compiler_params=_cparams("parallel"),
        name="nsa_prep",
    )(proj, cos, sin)


def _page_part(page_ref, kv, g):
    if len(page_ref.shape) == 2:
        c0 = kv * NSA_KV_W + g * NSA_DH
        return page_ref[:, c0:c0 + NSA_DH]
    return page_ref[:, kv, g, :]


def _cmp_relayout_kernel(pt_ref, src_ref, out_ref):
    del pt_ref
    per_page = PAGE_SIZE // CMP_BLOCK
    for kv in range(2):
        for g in range(NSA_KVH):
            out_ref[kv, g] = _page_part(src_ref, kv, g).reshape(per_page, CMP_BLOCK, NSA_DH)


def cmp_relayout(src_pages, page_table):
    bt, n_pages = page_table.shape
    per_page = PAGE_SIZE // CMP_BLOCK
    nb = n_pages * per_page
    tail = src_pages.shape[1:]
    zeros = (0,) * len(tail)
    out = pl.pallas_call(
        _cmp_relayout_kernel,
        grid_spec=pltpu.PrefetchScalarGridSpec(
            num_scalar_prefetch=1,
            grid=(bt, n_pages),
            in_specs=[pl.BlockSpec((None,) + tail, lambda b, p, pt: (pt[b, p],) + zeros)],
            out_specs=pl.BlockSpec((2, NSA_KVH, None, per_page, CMP_BLOCK, NSA_DH),
                                   lambda b, p, pt: (0, 0, b, p, 0, 0)),
        ),
        out_shape=jax.ShapeDtypeStruct((2, NSA_KVH, bt, nb, CMP_BLOCK, NSA_DH), F32),
        compiler_params=_cparams("parallel", "parallel"),
        name="cmp_relayout",
    )(page_table, src_pages)
    return out.reshape(2, NSA_KVH, bt * nb, CMP_BLOCK, NSA_DH)


def _compress_kernel(x_ref, pe_ref, w1_ref, w2_ref, cos_ref, sin_ref, o_ref):
    dh = NSA_DH
    acc = jnp.zeros(o_ref.shape, F32)
    for r in range(CMP_BLOCK):
        h = (x_ref[:, r, :] + pe_ref[r:r + 1, :]).astype(BF16)
        acc = acc + _dot(h, w1_ref[r * dh:(r + 1) * dh, :])
    y = _dot(jax.nn.gelu(acc).astype(BF16), w2_ref[...])
    o_ref[...] = _rope128(y, cos_ref[...], sin_ref[...])


def compress(xc, pe, w1, w2, cos, sin):
    _, g, r, _, _ = xc.shape
    kdim = CMP_BLOCK * NSA_DH
    bm = min(r, 512)
    assert r % bm == 0
    return pl.pallas_call(
        _compress_kernel,
        grid=(2, g, r // bm),
        in_specs=[
            pl.BlockSpec((None, None, bm, CMP_BLOCK, NSA_DH), lambda kv, gi, i: (kv, gi, i, 0, 0)),
            pl.BlockSpec((None, CMP_BLOCK, NSA_DH), lambda kv, gi, i: (kv, 0, 0)),
            pl.BlockSpec((None, kdim, NSA_DH), lambda kv, gi, i: (kv, 0, 0)),
            pl.BlockSpec((None, NSA_DH, NSA_DH), lambda kv, gi, i: (kv, 0, 0)),
            pl.BlockSpec((None, bm, NSA_DH), lambda kv, gi, i: (kv, i, 0)),
            pl.BlockSpec((None, bm, NSA_DH), lambda kv, gi, i: (kv, i, 0)),
        ],
        out_specs=pl.BlockSpec((None, None, bm, NSA_DH), lambda kv, gi, i: (kv, gi, i, 0)),
        out_shape=jax.ShapeDtypeStruct((2, g, r, NSA_DH), F32),
        compiler_params=_cparams("parallel", "parallel", "parallel"),
        name="compress",
    )(xc, pe, w1, w2, cos, sin)


def _cmp_branch(s, cv_list, row_groups, pos_rows, nbl):
    blk_end = (lax.broadcasted_iota(jnp.int32, (1, nbl), 1) + 1) * CMP_BLOCK - 1
    m_cmp = blk_end <= pos_rows
    s = jnp.where(m_cmp, s, NEG_INF)
    mx = jnp.max(s, axis=1, keepdims=True)
    e = jnp.where(m_cmp, jnp.exp(s - mx), 0.0)
    den = jnp.sum(e, axis=1, keepdims=True)
    p = e / jnp.where(den > 0.0, den, 1.0)
    pb = p.astype(BF16)
    outs = [_dot(pb[r0:r1], cv) for (r0, r1), cv in zip(row_groups, cv_list)]
    o = outs[0] if len(outs) == 1 else jnp.concatenate(outs, axis=0)
    return p, o


def _select_blocks(imp, pos_l, nsel):
    rows, nbl = imp.shape
    imp2 = imp + pltpu.roll(imp, nbl - 1, 1)
    lane = lax.broadcasted_iota(jnp.int32, (rows, nbl), 1)
    j = lane >> 1
    valid = jnp.where((lane & 1) == 0, j, nsel) < nsel
    cur = pos_l >> 6
    forced = jnp.where(j == 0, 1, jnp.where(j == cur, 1, jnp.where(j == cur - 1, 1, 0))) > 0
    future = j * SEL_BLOCK > pos_l
    score = jnp.where(forced, SEL_FORCE, jnp.where(future, -SEL_FORCE, imp2))
    score = jnp.where(valid, score, -3.0e38)
    rank = jnp.zeros((rows, nbl), F32)
    for i in range(nsel):
        col = score[:, 2 * i:2 * i + 1]
        tie = jnp.where(lane > 2 * i, 1.0, 0.0)
        rank = rank + jnp.where(col > score, 1.0, jnp.where(col == score, tie, 0.0))
    k_sel = min(N_SEL, nsel)
    return jnp.where(valid, jnp.where(rank < k_sel, 1.0, 0.0), 0.0)


def _expand_sel(sel_b, k0, tk, nbl):
    n_lane = lax.broadcasted_iota(jnp.int32, (nbl, tk), 0)
    key = k0 + lax.broadcasted_iota(jnp.int32, (nbl, tk), 1)
    e = jnp.where(n_lane == 2 * (key >> 6), 1.0, 0.0).astype(BF16)
    return _dot(sel_b, e)


def _nsa_prompt_kernel(q_ref, ck_ref, cv_ref, ks_ref, vs_ref, kw_ref, vw_ref, gate_ref, o_ref,
                       *, seq, nsel, tk):
    tq, rep, dh = NSA_TQ, NSA_REP, NSA_DH
    i = pl.program_id(2)
    l0 = i * tq
    qb = q_ref[...]
    q = jnp.concatenate([qb[:, r * dh:(r + 1) * dh] for r in range(rep)], axis=0)
    row = lax.broadcasted_iota(jnp.int32, (rep * tq, 1), 0)
    pos_rows = l0 + (row & (tq - 1))
    pos_l = l0 + lax.broadcasted_iota(jnp.int32, (tq, 1), 0)

    ck = ck_ref[...].astype(BF16)
    cv = cv_ref[...].astype(BF16)
    nbl = ck.shape[0]
    p, o_cmp = _cmp_branch(_dot_nt(q, ck), [cv], [(0, rep * tq)], pos_rows, nbl)
    imp = sum(p[r * tq:(r + 1) * tq] for r in range(rep))
    sel_b = _select_blocks(imp, pos_l, nsel).astype(BF16)

    def slc_step(t, carry):
        m_i, l_i, acc = carry
        k0 = pl.multiple_of(t * tk, tk)
        k = ks_ref[pl.ds(k0, tk), :].astype(BF16)
        v = vs_ref[pl.ds(k0, tk), :].astype(BF16)
        key = k0 + lax.broadcasted_iota(jnp.int32, (1, tk), 1)
        selexp = _expand_sel(sel_b, k0, tk, nbl)
        bias = jnp.where(jnp.where(key <= pos_l, selexp, 0.0) > 0.5, 0.0, NEG_INF)
        s = _dot_nt(q, k) + jnp.concatenate([bias] * rep, axis=0)
        m_new = jnp.maximum(m_i, jnp.max(s, axis=1, keepdims=True))
        alpha = jnp.exp(m_i - m_new)
        pe = jnp.exp(s - m_new)
        l_new = alpha * l_i + jnp.sum(pe, axis=1, keepdims=True)
        acc = alpha * acc + _dot(pe.astype(BF16), v)
        return m_new, l_new, acc

    n_kt = (l0 + tq + tk - 1) // tk
    init = (jnp.full((rep * tq, 1), NEG_INF, F32), jnp.zeros((rep * tq, 1), F32),
            jnp.zeros((rep * tq, dh), F32))
    _, l_f, acc = lax.fori_loop(0, n_kt, slc_step, init)
    o_slc = acc / l_f

    span = min(seq, WINDOW + tq)
    start = pl.multiple_of(jnp.maximum(l0 - WINDOW, 0), tq)
    kw = kw_ref[pl.ds(start, span), :].astype(BF16)
    vw = vw_ref[pl.ds(start, span), :].astype(BF16)
    key = start + lax.broadcasted_iota(jnp.int32, (1, span), 1)
    okw = jnp.where(key <= pos_l, jnp.where(key > pos_l - WINDOW, 1.0, 0.0), 0.0)
    bias = jnp.where(okw > 0.5, 0.0, NEG_INF)
    s = _dot_nt(q, kw) + jnp.concatenate([bias] * rep, axis=0)
    pw = jnp.exp(s - jnp.max(s, axis=1, keepdims=True))
    o_win = _dot(pw.astype(BF16), vw) / jnp.sum(pw, axis=1, keepdims=True)

    gt = jax.nn.sigmoid(gate_ref[...])
    for r in range(rep):
        rs = slice(r * tq, (r + 1) * tq)
        o_r = (gt[:, 3 * r:3 * r + 1] * o_cmp[rs] + gt[:, 3 * r + 1:3 * r + 2] * o_slc[rs]
               + gt[:, 3 * r + 2:3 * r + 3] * o_win[rs])
        o_ref[:, r * dh:(r + 1) * dh] = o_r.astype(o_ref.dtype)


def nsa_attn_prompt(q, ckv, new_slc, new_win, gates_g, *, bt, nsel):
    m = q.shape[0]
    L = m // bt
    tq = NSA_TQ
    assert L % tq == 0 and L >= WINDOW + tq
    nq = L // tq
    nbl = ckv.shape[3]
    tk = 512
    gw = NSA_REP * NSA_DH
    slc3 = new_slc.reshape(bt, L, NSA_ROW_W)
    win3 = new_win.reshape(bt, L, NSA_ROW_W)
    kv_spec = lambda off: pl.BlockSpec((None, L, NSA_DH), lambda b, g, i: (b, 0, off + g))
    return pl.pallas_call(
        functools.partial(_nsa_prompt_kernel, seq=L, nsel=nsel, tk=tk),
        grid=(bt, NSA_KVH, nq),
        in_specs=[
            pl.BlockSpec((tq, gw), lambda b, g, i: (b * nq + i, g)),
            pl.BlockSpec((None, None, None, nbl, NSA_DH), lambda b, g, i: (0, g, b, 0, 0)),
            pl.BlockSpec((None, None, None, nbl, NSA_DH), lambda b, g, i: (1, g, b, 0, 0)),
            kv_spec(0), kv_spec(NSA_KVH), kv_spec(0), kv_spec(NSA_KVH),
            pl.BlockSpec((None, tq, LANES), lambda b, g, i: (g, b * nq + i, 0)),
        ],
        out_specs=pl.BlockSpec((tq, gw), lambda b, g, i: (b * nq + i, g)),
        out_shape=jax.ShapeDtypeStruct((m, NSA_HEADS * NSA_DH), BF16),
        compiler_params=_cparams("parallel", "parallel", "arbitrary"),
        name="nsa_attn_prompt",
    )(q, ckv, ckv, slc3, slc3, win3, win3, gates_g)


def _nsa_sample_kernel(pt_ref, qg_ref, ck_ref, cv_ref, page_ref, new_ref, win_ref, gate_ref,
                       o_ref, sel_scr, m_scr, l_scr, acc_scr, ocmp_scr,
                       *, dseq, nsel, pos0, n_pages, n_new, n_win):
    del pt_ref
    G, rep = NSA_KVH, NSA_REP
    rows = G * dseq * rep
    rpg = dseq * rep
    p_id = pl.program_id(1)
    row = lax.broadcasted_iota(jnp.int32, (rows, 1), 0)
    pos_rows = pos0 + ((row >> 3) & (dseq - 1))
    nbl = sel_scr.shape[1]
    groups = [(g * rpg, (g + 1) * rpg) for g in range(G)]

    @pl.when(p_id == 0)
    def _():
        s = jnp.concatenate([_dot_nt(qg_ref[g], ck_ref[g].astype(BF16)) for g in range(G)], axis=0)
        p, o_cmp = _cmp_branch(s, [cv_ref[g].astype(BF16) for g in range(G)], groups, pos_rows, nbl)
        ocmp_scr[...] = o_cmp
        imp = jnp.sum(p.reshape(rows // rep, rep, nbl), axis=1, keepdims=True)
        imp = jnp.broadcast_to(imp, (rows // rep, rep, nbl)).reshape(rows, nbl)
        sel_scr[...] = _select_blocks(imp, pos_rows, nsel)
        m_scr[...] = jnp.full(m_scr.shape, NEG_INF, F32)
        l_scr[...] = jnp.zeros(l_scr.shape, F32)
        acc_scr[...] = jnp.zeros(acc_scr.shape, F32)

    def scores(src_ref):
        return jnp.concatenate(
            [_dot_nt(qg_ref[g], _page_part(src_ref, 0, g).astype(BF16)) for g in range(G)], axis=0)

    def weighted_values(pb, src_ref):
        return jnp.concatenate(
            [_dot(pb[r0:r1], _page_part(src_ref, 1, g).astype(BF16)) for g, (r0, r1) in enumerate(groups)],
            axis=0)

    def attend(src_ref, bias):
        s = scores(src_ref) + bias
        m_old = m_scr[...]
        m_new = jnp.maximum(m_old, jnp.max(s, axis=1, keepdims=True))
        alpha = jnp.exp(m_old - m_new)
        pe = jnp.where(bias == 0.0, jnp.exp(s - m_new), 0.0)
        l_scr[...] = alpha * l_scr[...] + jnp.sum(pe, axis=1, keepdims=True)
        acc_scr[...] = alpha * acc_scr[...] + weighted_values(pe.astype(BF16), src_ref)
        m_scr[...] = m_new

    k0 = p_id * PAGE_SIZE
    key = k0 + lax.broadcasted_iota(jnp.int32, (1, PAGE_SIZE), 1)
    selexp = _expand_sel(sel_scr[...].astype(BF16), k0, PAGE_SIZE, nbl)
    attend(page_ref, jnp.where(jnp.where(key <= pos_rows, selexp, 0.0) > 0.5, 0.0, NEG_INF))

    @pl.when(p_id == n_pages - 1)
    def _():
        past = n_pages * PAGE_SIZE
        nk = new_ref.shape[0]
        idx = lax.broadcasted_iota(jnp.int32, (1, nk), 1)
        selexp_n = _expand_sel(sel_scr[...].astype(BF16), past, nk, nbl)
        ok = jnp.where(idx < n_new, jnp.where(past + idx <= pos_rows, selexp_n, 0.0), 0.0)
        attend(new_ref, jnp.where(ok > 0.5, 0.0, NEG_INF))
        o_slc = acc_scr[...] / l_scr[...]
        nw = win_ref.shape[0]
        widx = lax.broadcasted_iota(jnp.int32, (1, nw), 1)
        kpos = pos0 - (n_win - n_new) + widx
        okw = jnp.where(widx < n_win, 1.0, 0.0)
        okw = jnp.where(kpos >= 0, okw, 0.0)
        okw = jnp.where(kpos <= pos_rows, okw, 0.0)
        okw = jnp.where(kpos > pos_rows - WINDOW, okw, 0.0)
        sw = scores(win_ref) + jnp.where(okw > 0.5, 0.0, NEG_INF)
        pw = jnp.exp(sw - jnp.max(sw, axis=1, keepdims=True))
        o_win = weighted_values(pw.astype(BF16), win_ref) / jnp.sum(pw, axis=1, keepdims=True)
        gt = jax.nn.sigmoid(gate_ref[...])
        o_ref[...] = gt[:, 0:1] * ocmp_scr[...] + gt[:, 1:2] * o_slc + gt[:, 2:3] * o_win


def nsa_attn_sample(page_table, qg, ckv, cache_slc, new_rows, win_rows, gates_s,
                    *, dseq, nsel, pos0, n_new, n_win):
    bt, n_pages = page_table.shape
    rows = NSA_KVH * dseq * NSA_REP
    nbl = ckv.shape[3]
    assert dseq & (dseq - 1) == 0 and NSA_REP == 8
    page_tail = cache_slc.shape[1:]
    return pl.pallas_call(
        functools.partial(_nsa_sample_kernel, dseq=dseq, nsel=nsel, pos0=pos0, n_pages=n_pages,
                          n_new=n_new, n_win=n_win),
        grid_spec=pltpu.PrefetchScalarGridSpec(
            num_scalar_prefetch=1,
            grid=(bt, n_pages),
            in_specs=[
                pl.BlockSpec((None, NSA_KVH, dseq * NSA_REP, NSA_DH), lambda b, p, pt: (b, 0, 0, 0)),
                pl.BlockSpec((None, NSA_KVH, None, nbl, NSA_DH), lambda b, p, pt: (0, 0, b, 0, 0)),
                pl.BlockSpec((None, NSA_KVH, None, nbl, NSA_DH), lambda b, p, pt: (1, 0, b, 0, 0)),
                pl.BlockSpec((None,) + page_tail, lambda b, p, pt: (pt[b, p],) + (0,) * len(page_tail)),
                pl.BlockSpec((None,) + new_rows.shape[1:], lambda b, p, pt: (b, 0, 0)),
                pl.BlockSpec((None,) + win_rows.shape[1:], lambda b, p, pt: (b, 0, 0)),
                pl.BlockSpec((None, rows, LANES), lambda b, p, pt: (b, 0, 0)),
            ],
            out_specs=pl.BlockSpec((None, rows, NSA_DH), lambda b, p, pt: (b, 0, 0)),
            scratch_shapes=[pltpu.VMEM((rows, nbl), F32), pltpu.VMEM((rows, 1), F32),
                            pltpu.VMEM((rows, 1), F32), pltpu.VMEM((rows, NSA_DH), F32),
                            pltpu.VMEM((rows, NSA_DH), F32)],
        ),
        out_shape=jax.ShapeDtypeStruct((bt, rows, NSA_DH), F32),
        compiler_params=_cparams("parallel", "arbitrary"),
        name="nsa_attn_sample",
    )(page_table, qg, ckv, ckv, cache_slc, new_rows, win_rows, gates_s)


def _round_up(x, mult):
    return -(-x // mult) * mult


def nsa_mixer(xb, w, *, bt, pos0, page_table=None, cache_cmp=None, cache_slc=None, cache_win=None):
    m = xb.shape[0]
    L = m // bt
    G, R, dh = NSA_KVH, NSA_REP, NSA_DH
    proj = mm(xb, w['in_main'])
    gates = mm(xb, w['in_gate'], bn=LANES)[:, :3 * NSA_HEADS]
    cos, sin = _rope_tables(pos0 + jnp.arange(L), dh)
    if L % 8:
        cos, sin = jnp.tile(cos, (bt, 1)), jnp.tile(sin, (bt, 1))
    q, new_cmp, new_slc, new_win = nsa_prep(proj, cos, sin, rows_per_table=cos.shape[0])
    past = 0 if page_table is None else page_table.shape[1] * PAGE_SIZE
    total = past + L
    tp = _round_up(total, SEL_BLOCK)
    nsel = tp // SEL_BLOCK
    nbl = _round_up(tp // CMP_BLOCK, LANES)

    if page_table is None:
        src = new_cmp.reshape(m // PAGE_SIZE, PAGE_SIZE, NSA_ROW_W)
        pt = jnp.arange(m // PAGE_SIZE, dtype=jnp.int32).reshape(bt, L // PAGE_SIZE)
    else:
        src = cache_cmp
        pt = page_table
    xc = cmp_relayout(src, pt)
    nb = xc.shape[2] // bt
    ccos, csin = _rope_tables((jnp.arange(nb) + 1) * CMP_BLOCK - 1, dh)
    ccos = jnp.stack([jnp.tile(ccos, (bt, 1)), jnp.ones((bt * nb, dh), F32)])
    csin = jnp.stack([jnp.tile(csin, (bt, 1)), jnp.zeros((bt * nb, dh), F32)])
    ckv = compress(xc, w['pe'], w['c1'], w['c2'], ccos, csin).reshape(2, G, bt, nb, dh)
    ckv = jnp.pad(ckv, ((0, 0), (0, 0), (0, 0), (0, nbl - nb), (0, 0)))

    g5 = gates.reshape(bt, L, G, R, 3)
    if page_table is None:
        gates_g = jnp.transpose(g5, (2, 0, 1, 3, 4)).reshape(G, m, R * 3)
        gates_g = jnp.pad(gates_g, ((0, 0), (0, 0), (0, LANES - R * 3)))
        o = nsa_attn_prompt(q, ckv, new_slc, new_win, gates_g, bt=bt, nsel=nsel)
        win_keep = new_win.reshape(bt, L, NSA_ROW_W)[:, L - min(WINDOW, L):]
    else:
        q5 = q.reshape(bt, L, G, R, dh)
        qg = jnp.transpose(q5, (0, 2, 1, 3, 4)).reshape(bt, G, L * R, dh)
        gates_s = jnp.transpose(g5, (0, 2, 1, 3, 4)).reshape(bt, G * L * R, 3)
        gates_s = jnp.pad(gates_s, ((0, 0), (0, 0), (0, LANES - 3)))
        new3 = new_slc.reshape(bt, L, NSA_ROW_W)
        new_pad = jnp.pad(new3, ((0, 0), (0, _round_up(L, LANES) - L), (0, 0)))
        prefix = cache_win.reshape(bt, -1, NSA_ROW_W)
        win_all = jnp.concatenate([prefix, new_win.reshape(bt, L, NSA_ROW_W)], axis=1)
        n_win = win_all.shape[1]
        win_pad = jnp.pad(win_all, ((0, 0), (0, _round_up(n_win, LANES) - n_win), (0, 0)))
        o = nsa_attn_sample(pt, qg, ckv, cache_slc, new_pad, win_pad,
                            gates_s, dseq=L, nsel=nsel, pos0=pos0, n_new=L, n_win=n_win)
        o = jnp.transpose(o.reshape(bt, G, L, R, dh), (0, 2, 1, 3, 4)).reshape(m, NSA_HEADS * dh).astype(BF16)
        win_keep = win_all[:, n_win - prefix.shape[1]:]
    shape5 = (bt, L, 2, G, dh)
    return o, new_cmp.reshape(shape5), new_slc.reshape(shape5), win_keep.reshape(bt, -1, 2, G, dh)


def kernel(x_prompt, x_sample, state_ssm, state_ret, cache_cmp_kv, cache_slc_kv, cache_win_kv, state_pool, page_table, ln_g, ln_b, ffn_w_gate, ffn_w_up, ffn_w_down, ssm_a_re, ssm_a_im, ssm_log_dt, ssm_b_re, ssm_b_im, ssm_c_re, ssm_c_im, ssm_d, ssm_w_glu, ret_w_in, ret_norm_g, ret_w_out, nsa_w_in, nsa_pe_ck, nsa_w_ck1, nsa_w_ck2, nsa_pe_cv, nsa_w_cv1, nsa_w_cv2, nsa_w_out, pool_w, pool_scale):
    past_len = page_table.shape[1] * PAGE_SIZE

    wg = ffn_w_gate
    wu = ffn_w_up
    wd = ffn_w_down.astype(BF16)
    w_glu = ssm_w_glu[0]
    r_w_in = ret_w_in[0]
    r_w_out = ret_w_out[0].astype(BF16)
    n_main = NSA_HEADS * NSA_DH + 6 * NSA_KV_W
    nsa_w = {
        'in_main': nsa_w_in[0, :, :n_main].astype(BF16),
        'in_gate': jnp.pad(nsa_w_in[0, :, n_main:], ((0, 0), (0, LANES - 3 * NSA_HEADS))).astype(BF16),
        'pe': jnp.stack([nsa_pe_ck[0], nsa_pe_cv[0]]).astype(F32),
        'c1': jnp.stack([nsa_w_ck1[0], nsa_w_cv1[0]]).astype(BF16),
        'c2': jnp.stack([nsa_w_ck2[0], nsa_w_cv2[0]]).astype(BF16),
    }
    n_w_out = nsa_w_out[0].astype(BF16)
    p_w = pool_w[0].astype(BF16)
    a_tab, bblk, cblk = _ssm_tables(ssm_a_re[0], ssm_a_im[0], ssm_log_dt[0], ssm_b_re[0], ssm_b_im[0],
                                    ssm_c_re[0], ssm_c_im[0])

    def ffn(xf, xb, i, s, ln_idx):
        h = dual_mm(xb, wg, wu, combine=_swiglu_combine, n=D_FF, widx_a=(i, s), widx_b=(i, s), bn=256)
        return mm_res_ln(h, wd, xf, ln_g[i, ln_idx], ln_b[i, ln_idx], widx=(i, s), scale=0.5)

    def stream(x, pos0, is_prompt):
        bt, L, _ = x.shape
        m = bt * L
        xf = x.astype(F32).reshape(m, D_MODEL)
        xb = xf.astype(BF16)
        outs = {}
        for i in range(DEPTH):
            mixer = i % N_MIXERS
            xf, xb = ffn(xf, xb, i, 0, 0)
            if mixer == 0:
                h0 = jnp.zeros((bt, SSM_GROUPS, SSM_N, 2), F32) if is_prompt else state_ssm[0]
                yg, outs['ssm'] = ssm_scan(xf, h0, a_tab, bblk, cblk, ssm_d[0], bt=bt)
                y = dual_mm(yg, w_glu, w_glu, combine=_glu_combine, n=D_MODEL, col0_b=D_MODEL,
                            out_dtype=F32, bn=256)
                xf, xb = res_ln(y, xf, ln_g[i, 1], ln_b[i, 1])
            else:
                if mixer == 1:
                    proj = mm(xb, r_w_in)
                    s0 = None if is_prompt else state_ret[0]
                    o, outs['ret'] = retention_core(proj, s0, ret_norm_g[0], bt=bt, pos0=pos0)
                    w_o = r_w_out
                elif mixer == 2:
                    if is_prompt:
                        o, c_new, s_new, w_new = nsa_mixer(xb, nsa_w, bt=bt, pos0=pos0)
                    else:
                        o, c_new, s_new, w_new = nsa_mixer(
                            xb, nsa_w, bt=bt, pos0=pos0, page_table=page_table,
                            cache_cmp=cache_cmp_kv[0], cache_slc=cache_slc_kv[0], cache_win=cache_win_kv[0])
                    outs['cmp'], outs['slc'], outs['win'] = c_new, s_new, w_new
                    w_o = n_w_out
                else:
                    prefix = jnp.zeros((bt, POOL_PAST, D_MODEL), F32) if is_prompt else state_pool[0]
                    y = pool_core(xf, prefix, p_w, pool_scale[0], bt=bt, pos0=pos0)
                    x3 = xf.reshape(bt, L, D_MODEL)
                    if L < POOL_PAST:
                        x3 = jnp.concatenate([prefix.astype(F32)[:, L:], x3], axis=1)
                    outs['pool'] = x3[:, -POOL_PAST:]
                if mixer == 3:
                    xf, xb = res_ln(y, xf, ln_g[i, 1], ln_b[i, 1])
                else:
                    xf, xb = mm_res_ln(o, w_o, xf, ln_g[i, 1], ln_b[i, 1])
            xf, xb = ffn(xf, xb, i, 1, 2)
        return xf.reshape(bt, L, D_MODEL), outs

    yp, op = stream(x_prompt, 0, True)
    ys, os_ = stream(x_sample, past_len, False)
    return (yp.astype(x_prompt.dtype), ys.astype(x_sample.dtype),
            op['ssm'][None], os_['ssm'][None], op['ret'][None], os_['ret'][None],
            op['cmp'][None], os_['cmp'][None], op['slc'][None], os_['slc'][None],
            op['win'][None], os_['win'][None], op['pool'][None], os_['pool'][None])
```

```python
import functools
import math

import jax
import jax.numpy as jnp
from jax import lax
from jax.experimental import pallas as pl
from jax.experimental.pallas import tpu as pltpu

F32 = jnp.float32
BF16 = jnp.bfloat16

D_MODEL = 4096
DEPTH = 4
PAGE_SIZE = 128
N_MIXERS = 4
D_FF = 2 * D_MODEL
DN_ALPHA = (2.0 * DEPTH) ** 0.25
LN_EPS = 1e-5
ROPE_THETA = 10000.0
NEG_INF = -1e30

SSM_GROUP = 16
SSM_GROUPS = D_MODEL // SSM_GROUP
SSM_N = 64
SSM_TILE_GROUPS = 16
SSM_TILE_CH = SSM_TILE_GROUPS * SSM_GROUP
SSM_TILE_ST = SSM_TILE_GROUPS * SSM_N

RET_DK = 256
RET_HEADS = D_MODEL // RET_DK
RET_DV = 2 * RET_DK
RET_CHUNK = 128

NSA_DH = 128
NSA_HEADS = D_MODEL // NSA_DH
NSA_KVH = 4
NSA_REP = NSA_HEADS // NSA_KVH
CMP_BLOCK = 32
SEL_BLOCK = 64
N_SEL = 16
WINDOW = 512
SEL_FORCE = 1e4
NSA_KV_W = NSA_KVH * NSA_DH
NSA_ROW_W = 2 * NSA_KV_W
NSA_TQ = 128

POOL_WINDOWS = (2, 4, 8, 16)
POOL_CH = D_MODEL // len(POOL_WINDOWS)
POOL_PAST = max(POOL_WINDOWS) - 1
POOL_HALO = 16

LANES = 128
V7X_VMEM_BYTES = 64 * 1024 * 1024
VMEM_LIMIT = V7X_VMEM_BYTES - 8 * 1024 * 1024


def _cparams(*sem):
    return pltpu.CompilerParams(dimension_semantics=sem, vmem_limit_bytes=VMEM_LIMIT)


def _dot(a, b):
    return jnp.dot(a, b, preferred_element_type=F32)


def _dot_nt(a, b):
    return lax.dot_general(a, b, (((1,), (1,)), ((), ())), preferred_element_type=F32)


def _dot_tn(a, b):
    return lax.dot_general(a, b, (((0,), (0,)), ((), ())), preferred_element_type=F32)


def _mm_kernel(x_ref, w_ref, o_ref):
    o_ref[...] = _dot(x_ref[...], w_ref[...].astype(BF16)).astype(o_ref.dtype)


def _w_spec(k, bn, col0_blocks, widx):
    lead = tuple(widx)
    shape = (None,) * len(lead) + (k, bn)
    return pl.BlockSpec(shape, lambda i, j: lead + (0, j + col0_blocks))


def mm(x, w, *, widx=(), n=None, col0=0, out_dtype=F32, bm=None, bn=512):
    m, k = x.shape
    n = w.shape[-1] if n is None else n
    if bm is None:
        bm = min(m, 1024 if k <= 4096 else 512)
    bn = min(bn, n)
    assert m % bm == 0 and n % bn == 0 and col0 % bn == 0
    return pl.pallas_call(
        _mm_kernel,
        grid=(m // bm, n // bn),
        in_specs=[pl.BlockSpec((bm, k), lambda i, j: (i, 0)),
                  _w_spec(k, bn, col0 // bn, widx)],
        out_specs=pl.BlockSpec((bm, bn), lambda i, j: (i, j)),
        out_shape=jax.ShapeDtypeStruct((m, n), out_dtype),
        compiler_params=_cparams("parallel", "arbitrary"),
        name="mm",
    )(x, w)


def _dual_mm_kernel(x_ref, wa_ref, wb_ref, o_ref, *, combine):
    x = x_ref[...]
    a = _dot(x, wa_ref[...].astype(BF16))
    b = _dot(x, wb_ref[...].astype(BF16))
    o_ref[...] = combine(a, b).astype(o_ref.dtype)


def _swiglu_combine(a, b):
    return a * jax.nn.sigmoid(a) * b


def _glu_combine(a, b):
    return a * jax.nn.sigmoid(b)


def dual_mm(x, wa, wb, *, combine, n, widx_a=(), widx_b=(), col0_a=0, col0_b=0,
            out_dtype=BF16, bm=None, bn=512):
    m, k = x.shape
    if bm is None:
        bm = min(m, 1024)
    assert m % bm == 0 and n % bn == 0 and col0_a % bn == 0 and col0_b % bn == 0
    return pl.pallas_call(
        functools.partial(_dual_mm_kernel, combine=combine),
        grid=(m // bm, n // bn),
        in_specs=[pl.BlockSpec((bm, k), lambda i, j: (i, 0)),
                  _w_spec(k, bn, col0_a // bn, widx_a),
                  _w_spec(k, bn, col0_b // bn, widx_b)],
        out_specs=pl.BlockSpec((bm, bn), lambda i, j: (i, j)),
        out_shape=jax.ShapeDtypeStruct((m, n), out_dtype),
        compiler_params=_cparams("parallel", "arbitrary"),
        name="dual_mm",
    )(x, wa, wb)


def _res_ln_kernel(y_ref, r_ref, g_ref, b_ref, of_ref, ob_ref, *, scale):
    z = DN_ALPHA * r_ref[...] + scale * y_ref[...].astype(F32)
    mu = jnp.mean(z, axis=-1, keepdims=True)
    zc = z - mu
    var = jnp.mean(zc * zc, axis=-1, keepdims=True)
    out = zc * lax.rsqrt(var + LN_EPS) * g_ref[...] + b_ref[...]
    of_ref[...] = out
    ob_ref[...] = out.astype(BF16)


def res_ln(y, resid, g, b, *, scale=1.0, bm=256):
    m, d = resid.shape
    bm = min(bm, m)
    assert m % bm == 0
    row = pl.BlockSpec((bm, d), lambda i: (i, 0))
    vec = pl.BlockSpec((1, d), lambda i: (0, 0))
    return pl.pallas_call(
        functools.partial(_res_ln_kernel, scale=scale),
        grid=(m // bm,),
        in_specs=[row, row, vec, vec],
        out_specs=[row, row],
        out_shape=[jax.ShapeDtypeStruct((m, d), F32), jax.ShapeDtypeStruct((m, d), BF16)],
        compiler_params=_cparams("parallel"),
        name="res_ln",
    )(y, resid, g.reshape(1, d), b.reshape(1, d))


def mm_res_ln(h, w, resid, g, b, *, widx=(), scale=1.0):
    return res_ln(mm(h, w, widx=widx, out_dtype=BF16), resid, g, b, scale=scale)


def _ssm_kernel(u_ref, h0_ref, a_ref, bblk_ref, cblk_ref, d_ref, y_ref, hout_ref,
                h_scr, u_scr, y_scr, bu_scr, hs_scr, *, bt, tc, nchunks):
    c = pl.program_id(1)
    st = SSM_TILE_ST
    nl = SSM_TILE_CH // LANES

    @pl.when(c == 0)
    def _():
        h_scr[...] = jnp.zeros_like(h_scr)
        h_scr[0:bt, :] = h0_ref[...]

    for b in range(bt):
        for j in range(nl):
            u_scr[j, pl.ds(b, tc, stride=bt), :] = u_ref[b, :, j * LANES:(j + 1) * LANES]
    u = jnp.concatenate([u_scr[j] for j in range(nl)], axis=1)
    bu_scr[...] = _dot(u.astype(BF16), bblk_ref[...])
    ar = a_ref[0:1, :]
    ai = a_ref[1:2, :]

    def advance(hr, hi, sr, si):
        return ar * hr - ai * hi + sr, ar * hi + ai * hr + si

    if bt == 8:
        def step(t, carry):
            r0 = pl.multiple_of(t * 8, 8)
            hr, hi = advance(*carry, bu_scr[pl.ds(r0, 8), 0:st], bu_scr[pl.ds(r0, 8), st:2 * st])
            hs_scr[pl.ds(r0, 8), 0:st] = hr
            hs_scr[pl.ds(r0, 8), st:2 * st] = hi
            return hr, hi
        n_it = tc
    else:
        low = lax.broadcasted_iota(jnp.int32, (8, st), 0) < bt

        def step(t2, carry):
            r0 = pl.multiple_of(t2 * 8, 8)
            sr = bu_scr[pl.ds(r0, 8), 0:st]
            si = bu_scr[pl.ds(r0, 8), st:2 * st]
            e_r, e_i = advance(*carry, sr, si)
            o_r, o_i = advance(pltpu.roll(e_r, bt, 0), pltpu.roll(e_i, bt, 0), sr, si)
            hs_scr[pl.ds(r0, 8), 0:st] = jnp.where(low, e_r, o_r)
            hs_scr[pl.ds(r0, 8), st:2 * st] = jnp.where(low, e_i, o_i)
            return pltpu.roll(o_r, bt, 0), pltpu.roll(o_i, bt, 0)
        n_it = tc // 2

    hr, hi = lax.fori_loop(0, n_it, step, (h_scr[:, 0:st], h_scr[:, st:2 * st]))
    h_scr[:, 0:st] = hr
    h_scr[:, st:2 * st] = hi
    y = jax.nn.gelu(_dot(hs_scr[...].astype(BF16), cblk_ref[...]) + u * d_ref[...])
    for j in range(nl):
        y_scr[j] = y[:, j * LANES:(j + 1) * LANES]
    for b in range(bt):
        for j in range(nl):
            y_ref[b, :, j * LANES:(j + 1) * LANES] = y_scr[j, pl.ds(b, tc, stride=bt), :].astype(y_ref.dtype)

    @pl.when(c == nchunks - 1)
    def _():
        hout_ref[...] = h_scr[0:bt, :]


def _ssm_tables(a_re, a_im, log_dt, b_re, b_im, c_re, c_im):
    ar = a_re.astype(F32)
    ai = a_im.astype(F32)
    dt = jnp.exp(log_dt.astype(F32))[:, None]
    mag = jnp.exp(ar * dt)
    abar_r = mag * jnp.cos(ai * dt)
    abar_i = mag * jnp.sin(ai * dt)
    den = ar * ar + ai * ai
    zr = abar_r - 1.0
    coef_r = (zr * ar + abar_i * ai) / den
    coef_i = (abar_i * ar - zr * ai) / den
    br = b_re.astype(F32)
    bi = b_im.astype(F32)
    bbar_r = coef_r[..., None] * br - coef_i[..., None] * bi
    bbar_i = coef_r[..., None] * bi + coef_i[..., None] * br
    gt = SSM_TILE_GROUPS
    nt = SSM_GROUPS // gt
    eye = jnp.eye(gt, dtype=F32)
    bb = jnp.stack([bbar_r, bbar_i]).reshape(2, nt, gt, SSM_N, SSM_GROUP)
    bblk = jnp.einsum('rtgnc,gh->tgcrhn', bb, eye).reshape(nt, SSM_TILE_CH, 2 * SSM_TILE_ST)
    cc = jnp.stack([c_re.astype(F32), -c_im.astype(F32)]).reshape(2, nt, gt, SSM_GROUP, SSM_N)
    cblk = jnp.einsum('rtgcn,gh->trgnhc', cc, eye).reshape(nt, 2 * SSM_TILE_ST, SSM_TILE_CH)
    a_tab = jnp.stack([abar_r.reshape(nt, SSM_TILE_ST), abar_i.reshape(nt, SSM_TILE_ST)], axis=1)
    return a_tab, bblk.astype(BF16), cblk.astype(BF16)


def ssm_scan(x, h0, a_tab, bblk, cblk, d_skip, *, bt):
    m, d = x.shape
    L = m // bt
    assert bt in (4, 8)
    nt = SSM_GROUPS // SSM_TILE_GROUPS
    tc = min(L, 128)
    assert L % tc == 0 and (tc * bt) % 16 == 0
    nchunks = L // tc
    rows = tc * bt
    h0t = h0.astype(F32).reshape(bt, nt, SSM_TILE_GROUPS, SSM_N, 2)
    h0t = jnp.transpose(h0t, (1, 0, 4, 2, 3)).reshape(nt, bt, 2 * SSM_TILE_ST)
    tok = pl.BlockSpec((bt, tc, SSM_TILE_CH), lambda j, c: (0, c, j))
    y, hout = pl.pallas_call(
        functools.partial(_ssm_kernel, bt=bt, tc=tc, nchunks=nchunks),
        grid=(nt, nchunks),
        in_specs=[
            tok,
            pl.BlockSpec((None, bt, 2 * SSM_TILE_ST), lambda j, c: (j, 0, 0)),
            pl.BlockSpec((None, 2, SSM_TILE_ST), lambda j, c: (j, 0, 0)),
            pl.BlockSpec((None, SSM_TILE_CH, 2 * SSM_TILE_ST), lambda j, c: (j, 0, 0)),
            pl.BlockSpec((None, 2 * SSM_TILE_ST, SSM_TILE_CH), lambda j, c: (j, 0, 0)),
            pl.BlockSpec((1, SSM_TILE_CH), lambda j, c: (0, j)),
        ],
        out_specs=[tok, pl.BlockSpec((None, bt, 2 * SSM_TILE_ST), lambda j, c: (j, 0, 0))],
        out_shape=[jax.ShapeDtypeStruct((bt, L, d), BF16),
                   jax.ShapeDtypeStruct((nt, bt, 2 * SSM_TILE_ST), F32)],
        scratch_shapes=[pltpu.VMEM((8, 2 * SSM_TILE_ST), F32),
                        pltpu.VMEM((SSM_TILE_CH // LANES, rows, LANES), F32),
                        pltpu.VMEM((SSM_TILE_CH // LANES, rows, LANES), F32),
                        pltpu.VMEM((rows, 2 * SSM_TILE_ST), F32),
                        pltpu.VMEM((rows, 2 * SSM_TILE_ST), F32)],
        compiler_params=_cparams("parallel", "arbitrary"),
        name="ssm_scan",
    )(x.reshape(bt, L, d), h0t, a_tab, bblk, cblk, d_skip.astype(F32).reshape(1, d))
    hfin = hout.reshape(nt, bt, 2, SSM_TILE_GROUPS, SSM_N)
    hfin = jnp.transpose(hfin, (1, 0, 3, 4, 2)).reshape(bt, SSM_GROUPS, SSM_N, 2)
    return y.reshape(m, d), hfin


def _ret_kernel(*refs, nc, zero_init):
    if zero_init:
        (q_ref, k_ref, v_ref, g_ref, cos_ref, sin_ref, intra_ref, qd_ref, kd_ref, cd_ref, ng_ref,
         o_ref, sfin_ref, s_scr) = refs
        s0_ref = None
    else:
        (q_ref, k_ref, v_ref, g_ref, cos_ref, sin_ref, intra_ref, qd_ref, kd_ref, cd_ref, ng_ref,
         s0_ref, o_ref, sfin_ref, s_scr) = refs
    c = pl.program_id(2)

    @pl.when(c == 0)
    def _():
        if zero_init:
            s_scr[...] = jnp.zeros_like(s_scr)
        else:
            s_scr[...] = s0_ref[...]

    cos = cos_ref[...]
    sin = sin_ref[...]
    half = RET_DK // 2

    def rope(x):
        x1 = x[:, :half]
        x2 = x[:, half:]
        return jnp.concatenate([x1 * cos - x2 * sin, x2 * cos + x1 * sin], axis=1)

    q = rope(q_ref[...])
    k = rope(k_ref[...]) * RET_DK ** -0.5
    qb = q.astype(BF16)
    vb = v_ref[...].astype(BF16)
    sc = _dot_nt(qb, k.astype(BF16)) * intra_ref[...]
    s_old = s_scr[...]
    o = _dot(sc.astype(BF16), vb) + _dot(qb, s_old.astype(BF16)) * qd_ref[...]
    s_new = s_old * cd_ref[...] + _dot_tn((k * kd_ref[...]).astype(BF16), vb)
    s_scr[...] = s_new
    mu = jnp.mean(o, axis=-1, keepdims=True)
    oc = o - mu
    var = jnp.mean(oc * oc, axis=-1, keepdims=True)
    on = oc * lax.rsqrt(var + LN_EPS) * ng_ref[...]
    g = g_ref[...]
    o_ref[...] = (g * jax.nn.sigmoid(g) * on).astype(o_ref.dtype)

    @pl.when(c == nc - 1)
    def _():
        sfin_ref[...] = s_new


def retention_core(proj, s0, norm_g, *, bt, pos0):
    m, n = proj.shape
    L = m // bt
    H = RET_HEADS
    C = math.gcd(L, RET_CHUNK)
    nc = L // C
    pos = (pos0 + jnp.arange(L)).astype(F32)
    inv = 1.0 / (ROPE_THETA ** jnp.linspace(0.0, 1.0, RET_DK // 2, dtype=F32))
    ang = pos[:, None] * inv[None, :]
    cos_t, sin_t = jnp.cos(ang), jnp.sin(ang)
    log_g = jnp.log1p(-jnp.exp2(-5.0 - jnp.arange(H, dtype=F32)))
    idx = jnp.arange(C, dtype=F32)
    diff = idx[:, None] - idx[None, :]
    intra = jnp.where(diff >= 0, jnp.exp(log_g[:, None, None] * jnp.maximum(diff, 0.0)), 0.0)
    q_dec = jnp.exp(log_g[:, None] * (idx[None, :] + 1.0))[..., None]
    k_dec = jnp.exp(log_g[:, None] * (C - 1.0 - idx[None, :]))[..., None]
    chunk_dec = jnp.exp(log_g * C).reshape(H, 1, 1)
    zero_init = s0 is None
    proj3 = proj.reshape(bt, L, n)
    kb0 = H
    vb0 = 2 * H * RET_DK // RET_DV
    gb0 = vb0 + H
    in_specs = [
        pl.BlockSpec((None, C, RET_DK), lambda b, h, c: (b, c, h)),
        pl.BlockSpec((None, C, RET_DK), lambda b, h, c: (b, c, kb0 + h)),
        pl.BlockSpec((None, C, RET_DV), lambda b, h, c: (b, c, vb0 + h)),
        pl.BlockSpec((None, C, RET_DV), lambda b, h, c: (b, c, gb0 + h)),
        pl.BlockSpec((C, RET_DK // 2), lambda b, h, c: (c, 0)),
        pl.BlockSpec((C, RET_DK // 2), lambda b, h, c: (c, 0)),
        pl.BlockSpec((None, C, C), lambda b, h, c: (h, 0, 0)),
        pl.BlockSpec((None, C, 1), lambda b, h, c: (h, 0, 0)),
        pl.BlockSpec((None, C, 1), lambda b, h, c: (h, 0, 0)),
        pl.BlockSpec((None, 1, 1), lambda b, h, c: (h, 0, 0)),
        pl.BlockSpec((1, RET_DV), lambda b, h, c: (0, h)),
    ]
    args = [proj3, proj3, proj3, proj3, cos_t, sin_t, intra, q_dec, k_dec, chunk_dec,
            norm_g.astype(F32).reshape(1, H * RET_DV)]
    st_spec = pl.BlockSpec((None, None, RET_DK, RET_DV), lambda b, h, c: (b, h, 0, 0))
    if not zero_init:
        in_specs.append(st_spec)
        args.append(s0.astype(F32))
    o, sfin = pl.pallas_call(
        functools.partial(_ret_kernel, nc=nc, zero_init=zero_init),
        grid=(bt, H, nc),
        in_specs=in_specs,
        out_specs=[pl.BlockSpec((None, C, RET_DV), lambda b, h, c: (b, c, h)), st_spec],
        out_shape=[jax.ShapeDtypeStruct((bt, L, H * RET_DV), BF16),
                   jax.ShapeDtypeStruct((bt, H, RET_DK, RET_DV), F32)],
        scratch_shapes=[pltpu.VMEM((RET_DK, RET_DV), F32)],
        compiler_params=_cparams("parallel", "parallel", "arbitrary"),
        name="retention_core",
    )(*args)
    return o.reshape(m, H * RET_DV), sfin


def _pool_kernel(*refs, tile, pos0, has_halo):
    if has_halo:
        x_ref, halo_ref, pre_ref, w_ref, sc_ref, y_ref, xx_scr = refs
    else:
        x_ref, pre_ref, w_ref, sc_ref, y_ref, xx_scr = refs
        halo_ref = None
    i = pl.program_id(1)
    hl = POOL_HALO
    if has_halo:
        @pl.when(i == 0)
        def _():
            xx_scr[0:hl, :] = pre_ref[...]

        @pl.when(i > 0)
        def _():
            xx_scr[0:hl, :] = halo_ref[...]
    else:
        xx_scr[0:hl, :] = pre_ref[...]
    xx_scr[hl:hl + tile, :] = x_ref[...]
    pos = pos0 + i * tile + lax.broadcasted_iota(jnp.int32, (tile, 1), 0)
    for gi, w in enumerate(POOL_WINDOWS):
        c0, c1 = gi * POOL_CH, (gi + 1) * POOL_CH
        xs = xx_scr[hl:hl + tile, c0:c1]
        acc = xs
        for k in range(1, w):
            acc = acc + xx_scr[hl - k:hl - k + tile, c0:c1]
        cnt = jnp.minimum(w, pos + 1).astype(F32)
        dlt = (acc / cnt - xs).astype(BF16)
        y_ref[:, c0:c1] = _dot(dlt, w_ref[gi]) * sc_ref[:, c0:c1]


def pool_core(xf, prefix, w_pool, scale, *, bt, pos0):
    m, d = xf.shape
    L = m // bt
    tile = min(L, 256)
    assert L % tile == 0
    has_halo = L > tile
    x3 = xf.reshape(bt, L, d)
    pre = jnp.pad(prefix.astype(F32), ((0, 0), (POOL_HALO - POOL_PAST, 0), (0, 0)))
    hb = tile // POOL_HALO
    row = pl.BlockSpec((None, tile, d), lambda b, i: (b, i, 0))
    in_specs = [row]
    args = [x3]
    if has_halo:
        in_specs.append(pl.BlockSpec((None, POOL_HALO, d), lambda b, i: (b, jnp.maximum(i * hb - 1, 0), 0)))
        args.append(x3)
    in_specs += [
        pl.BlockSpec((None, POOL_HALO, d), lambda b, i: (b, 0, 0)),
        pl.BlockSpec((len(POOL_WINDOWS), POOL_CH, POOL_CH), lambda b, i: (0, 0, 0)),
        pl.BlockSpec((1, d), lambda b, i: (0, 0)),
    ]
    args += [pre, w_pool, scale.astype(F32).reshape(1, d)]
    y = pl.pallas_call(
        functools.partial(_pool_kernel, tile=tile, pos0=pos0, has_halo=has_halo),
        grid=(bt, L // tile),
        in_specs=in_specs,
        out_specs=row,
        out_shape=jax.ShapeDtypeStruct((bt, L, d), F32),
        scratch_shapes=[pltpu.VMEM((POOL_HALO + tile, d), F32)],
        compiler_params=_cparams("parallel", "arbitrary"),
        name="pool_core",
    )(*args)
    return y.reshape(m, d)


def _rope_tables(pos, dim):
    inv = 1.0 / (ROPE_THETA ** (jnp.arange(0, dim, 2, dtype=F32) / dim))
    ang = pos.astype(F32)[:, None] * inv[None, :]
    cos = jnp.cos(ang)
    sin = jnp.sin(ang)
    return jnp.concatenate([cos, cos], axis=1), jnp.concatenate([-sin, sin], axis=1)


def _rope128(x, cos, sin):
    return x * cos + pltpu.roll(x, NSA_DH // 2, 1) * sin


def _nsa_prep_kernel(p_ref, cos_ref, sin_ref, q_ref, cmp_ref, slc_ref, win_ref):
    cos = cos_ref[...]
    sin = sin_ref[...]
    dh = NSA_DH
    qw = NSA_HEADS * dh
    for h in range(NSA_HEADS):
        x = p_ref[:, h * dh:(h + 1) * dh]
        q_ref[:, h * dh:(h + 1) * dh] = (_rope128(x, cos, sin) * dh ** -0.5).astype(q_ref.dtype)
    kvw = NSA_KV_W
    cmp_ref[...] = p_ref[:, qw:qw + 2 * kvw]
    for dst, base in ((slc_ref, qw + 2 * kvw), (win_ref, qw + 4 * kvw)):
        for g in range(NSA_KVH):
            dst[:, g * dh:(g + 1) * dh] = _rope128(p_ref[:, base + g * dh:base + (g + 1) * dh], cos, sin)
        dst[:, kvw:2 * kvw] = p_ref[:, base + kvw:base + 2 * kvw]


def nsa_prep(proj, cos, sin, *, rows_per_table):
    m, n = proj.shape
    bm = min(m, 256)
    assert m % bm == 0 and rows_per_table % bm == 0
    nt = rows_per_table // bm
    row = lambda w: pl.BlockSpec((bm, w), lambda i: (i, 0))
    tab = pl.BlockSpec((bm, NSA_DH), lambda i: (i % nt, 0))
    return pl.pallas_call(
        _nsa_prep_kernel,
        grid=(m // bm,),
        in_specs=[row(n), tab, tab],
        out_specs=[row(NSA_HEADS * NSA_DH), row(NSA_ROW_W), row(NSA_ROW_W), row(NSA_ROW_W)],
        out_shape=[jax.ShapeDtypeStruct((m, NSA_HEADS * NSA_DH), BF16)]
        + [jax.ShapeDtypeStruct((m, NSA_ROW_W), F32)] * 3,
        compiler_params=_cparams("parallel"),
        name="nsa_prep",
    )(proj, cos, sin)


def _page_part(page_ref, kv, g):
    if len(page_ref.shape) == 2:
        c0 = kv * NSA_KV_W + g * NSA_DH
        return page_ref[:, c0:c0 + NSA_DH]
    return page_ref[:, kv, g, :]


def _cmp_relayout_kernel(pt_ref, src_ref, out_ref):
    del pt_ref
    per_page = PAGE_SIZE // CMP_BLOCK
    for kv in range(2):
        for g in range(NSA_KVH):
            out_ref[kv, g] = _page_part(src_ref, kv, g).reshape(per_page, CMP_BLOCK, NSA_DH)


def cmp_relayout(src_pages, page_table):
    bt, n_pages = page_table.shape
    per_page = PAGE_SIZE // CMP_BLOCK
    nb = n_pages * per_page
    tail = src_pages.shape[1:]
    zeros = (0,) * len(tail)
    out = pl.pallas_call(
        _cmp_relayout_kernel,
        grid_spec=pltpu.PrefetchScalarGridSpec(
            num_scalar_prefetch=1,
            grid=(bt, n_pages),
            in_specs=[pl.BlockSpec((None,) + tail, lambda b, p, pt: (pt[b, p],) + zeros)],
            out_specs=pl.BlockSpec((2, NSA_KVH, None, per_page, CMP_BLOCK, NSA_DH),
                                   lambda b, p, pt: (0, 0, b, p, 0, 0)),
        ),
        out_shape=jax.ShapeDtypeStruct((2, NSA_KVH, bt, nb, CMP_BLOCK, NSA_DH), F32),
        compiler_params=_cparams("parallel", "parallel"),
        name="cmp_relayout",
    )(page_table, src_pages)
    return out.reshape(2, NSA_KVH, bt * nb, CMP_BLOCK, NSA_DH)


def _compress_kernel(x_ref, pe_ref, w1_ref, w2_ref, cos_ref, sin_ref, o_ref):
    dh = NSA_DH
    acc = jnp.zeros(o_ref.shape, F32)
    for r in range(CMP_BLOCK):
        h = (x_ref[:, r, :] + pe_ref[r:r + 1, :]).astype(BF16)
        acc = acc + _dot(h, w1_ref[r * dh:(r + 1) * dh, :])
    y = _dot(jax.nn.gelu(acc).astype(BF16), w2_ref[...])
    o_ref[...] = _rope128(y, cos_ref[...], sin_ref[...])


def compress(xc, pe, w1, w2, cos, sin):
    _, g, r, _, _ = xc.shape
    kdim = CMP_BLOCK * NSA_DH
    bm = min(r, 512)
    assert r % bm == 0
    return pl.pallas_call(
        _compress_kernel,
        grid=(2, g, r // bm),
        in_specs=[
            pl.BlockSpec((None, None, bm, CMP_BLOCK, NSA_DH), lambda kv, gi, i: (kv, gi, i, 0, 0)),
            pl.BlockSpec((None, CMP_BLOCK, NSA_DH), lambda kv, gi, i: (kv, 0, 0)),
            pl.BlockSpec((None, kdim, NSA_DH), lambda kv, gi, i: (kv, 0, 0)),
            pl.BlockSpec((None, NSA_DH, NSA_DH), lambda kv, gi, i: (kv, 0, 0)),
            pl.BlockSpec((None, bm, NSA_DH), lambda kv, gi, i: (kv, i, 0)),
            pl.BlockSpec((None, bm, NSA_DH), lambda kv, gi, i: (kv, i, 0)),
        ],
        out_specs=pl.BlockSpec((None, None, bm, NSA_DH), lambda kv, gi, i: (kv, gi, i, 0)),
        out_shape=jax.ShapeDtypeStruct((2, g, r, NSA_DH), F32),
        compiler_params=_cparams("parallel", "parallel", "parallel"),
        name="compress",
    )(xc, pe, w1, w2, cos, sin)


def _cmp_branch(s, cv_list, row_groups, pos_rows, nbl):
    blk_end = (lax.broadcasted_iota(jnp.int32, (1, nbl), 1) + 1) * CMP_BLOCK - 1
    m_cmp = blk_end <= pos_rows
    s = jnp.where(m_cmp, s, NEG_INF)
    mx = jnp.max(s, axis=1, keepdims=True)
    e = jnp.where(m_cmp, jnp.exp(s - mx), 0.0)
    den = jnp.sum(e, axis=1, keepdims=True)
    p = e / jnp.where(den > 0.0, den, 1.0)
    pb = p.astype(BF16)
    outs = [_dot(pb[r0:r1], cv) for (r0, r1), cv in zip(row_groups, cv_list)]
    o = outs[0] if len(outs) == 1 else jnp.concatenate(outs, axis=0)
    return p, o


def _select_blocks(imp, pos_l, nsel):
    rows, nbl = imp.shape
    imp2 = imp + pltpu.roll(imp, nbl - 1, 1)
    lane = lax.broadcasted_iota(jnp.int32, (rows, nbl), 1)
    j = lane >> 1
    valid = jnp.where((lane & 1) == 0, j, nsel) < nsel
    cur = pos_l >> 6
    forced = jnp.where(j == 0, 1, jnp.where(j == cur, 1, jnp.where(j == cur - 1, 1, 0))) > 0
    future = j * SEL_BLOCK > pos_l
    score = jnp.where(forced, SEL_FORCE, jnp.where(future, -SEL_FORCE, imp2))
    score = jnp.where(valid, score, -3.0e38)
    rank = jnp.zeros((rows, nbl), F32)
    for i in range(nsel):
        col = score[:, 2 * i:2 * i + 1]
        tie = jnp.where(lane > 2 * i, 1.0, 0.0)
        rank = rank + jnp.where(col > score, 1.0, jnp.where(col == score, tie, 0.0))
    k_sel = min(N_SEL, nsel)
    return jnp.where(valid, jnp.where(rank < k_sel, 1.0, 0.0), 0.0)


def _expand_sel(sel_b, k0, tk, nbl):
    n_lane = lax.broadcasted_iota(jnp.int32, (nbl, tk), 0)
    key = k0 + lax.broadcasted_iota(jnp.int32, (nbl, tk), 1)
    e = jnp.where(n_lane == 2 * (key >> 6), 1.0, 0.0).astype(BF16)
    return _dot(sel_b, e)


def _nsa_prompt_kernel(q_ref, ck_ref, cv_ref, ks_ref, vs_ref, kw_ref, vw_ref, gate_ref, o_ref,
                       *, seq, nsel, tk):
    tq, rep, dh = NSA_TQ, NSA_REP, NSA_DH
    i = pl.program_id(2)
    l0 = i * tq
    qb = q_ref[...]
    q = jnp.concatenate([qb[:, r * dh:(r + 1) * dh] for r in range(rep)], axis=0)
    row = lax.broadcasted_iota(jnp.int32, (rep * tq, 1), 0)
    pos_rows = l0 + (row & (tq - 1))
    pos_l = l0 + lax.broadcasted_iota(jnp.int32, (tq, 1), 0)

    ck = ck_ref[...].astype(BF16)
    cv = cv_ref[...].astype(BF16)
    nbl = ck.shape[0]
    p, o_cmp = _cmp_branch(_dot_nt(q, ck), [cv], [(0, rep * tq)], pos_rows, nbl)
    imp = sum(p[r * tq:(r + 1) * tq] for r in range(rep))
    sel_b = _select_blocks(imp, pos_l, nsel).astype(BF16)

    def slc_step(t, carry):
        m_i, l_i, acc = carry
        k0 = pl.multiple_of(t * tk, tk)
        k = ks_ref[pl.ds(k0, tk), :].astype(BF16)
        v = vs_ref[pl.ds(k0, tk), :].astype(BF16)
        key = k0 + lax.broadcasted_iota(jnp.int32, (1, tk), 1)
        selexp = _expand_sel(sel_b, k0, tk, nbl)
        bias = jnp.where(jnp.where(key <= pos_l, selexp, 0.0) > 0.5, 0.0, NEG_INF)
        s = _dot_nt(q, k) + jnp.concatenate([bias] * rep, axis=0)
        m_new = jnp.maximum(m_i, jnp.max(s, axis=1, keepdims=True))
        alpha = jnp.exp(m_i - m_new)
        pe = jnp.exp(s - m_new)
        l_new = alpha * l_i + jnp.sum(pe, axis=1, keepdims=True)
        acc = alpha * acc + _dot(pe.astype(BF16), v)
        return m_new, l_new, acc

    n_kt = (l0 + tq + tk - 1) // tk
    init = (jnp.full((rep * tq, 1), NEG_INF, F32), jnp.zeros((rep * tq, 1), F32),
            jnp.zeros((rep * tq, dh), F32))
    _, l_f, acc = lax.fori_loop(0, n_kt, slc_step, init)
    o_slc = acc / l_f

    span = min(seq, WINDOW + tq)
    start = pl.multiple_of(jnp.maximum(l0 - WINDOW, 0), tq)
    kw = kw_ref[pl.ds(start, span), :].astype(BF16)
    vw = vw_ref[pl.ds(start, span), :].astype(BF16)
    key = start + lax.broadcasted_iota(jnp.int32, (1, span), 1)
    okw = jnp.where(key <= pos_l, jnp.where(key > pos_l - WINDOW, 1.0, 0.0), 0.0)
    bias = jnp.where(okw > 0.5, 0.0, NEG_INF)
    s = _dot_nt(q, kw) + jnp.concatenate([bias] * rep, axis=0)
    pw = jnp.exp(s - jnp.max(s, axis=1, keepdims=True))
    o_win = _dot(pw.astype(BF16), vw) / jnp.sum(pw, axis=1, keepdims=True)

    gt = jax.nn.sigmoid(gate_ref[...])
    for r in range(rep):
        rs = slice(r * tq, (r + 1) * tq)
        o_r = (gt[:, 3 * r:3 * r + 1] * o_cmp[rs] + gt[:, 3 * r + 1:3 * r + 2] * o_slc[rs]
               + gt[:, 3 * r + 2:3 * r + 3] * o_win[rs])
        o_ref[:, r * dh:(r + 1) * dh] = o_r.astype(o_ref.dtype)


def nsa_attn_prompt(q, ckv, new_slc, new_win, gates_g, *, bt, nsel):
    m = q.shape[0]
    L = m // bt
    tq = NSA_TQ
    assert L % tq == 0 and L >= WINDOW + tq
    nq = L // tq
    nbl = ckv.shape[3]
    tk = 512
    gw = NSA_REP * NSA_DH
    slc3 = new_slc.reshape(bt, L, NSA_ROW_W)
    win3 = new_win.reshape(bt, L, NSA_ROW_W)
    kv_spec = lambda off: pl.BlockSpec((None, L, NSA_DH), lambda b, g, i: (b, 0, off + g))
    return pl.pallas_call(
        functools.partial(_nsa_prompt_kernel, seq=L, nsel=nsel, tk=tk),
        grid=(bt, NSA_KVH, nq),
        in_specs=[
            pl.BlockSpec((tq, gw), lambda b, g, i: (b * nq + i, g)),
            pl.BlockSpec((None, None, None, nbl, NSA_DH), lambda b, g, i: (0, g, b, 0, 0)),
            pl.BlockSpec((None, None, None, nbl, NSA_DH), lambda b, g, i: (1, g, b, 0, 0)),
            kv_spec(0), kv_spec(NSA_KVH), kv_spec(0), kv_spec(NSA_KVH),
            pl.BlockSpec((None, tq, LANES), lambda b, g, i: (g, b * nq + i, 0)),
        ],
        out_specs=pl.BlockSpec((tq, gw), lambda b, g, i: (b * nq + i, g)),
        out_shape=jax.ShapeDtypeStruct((m, NSA_HEADS * NSA_DH), BF16),
        compiler_params=_cparams("parallel", "parallel", "arbitrary"),
        name="nsa_attn_prompt",
    )(q, ckv, ckv, slc3, slc3, win3, win3, gates_g)


def _nsa_sample_kernel(pt_ref, qg_ref, ck_ref, cv_ref, page_ref, new_ref, win_ref, gate_ref,
                       o_ref, sel_scr, m_scr, l_scr, acc_scr, ocmp_scr,
                       *, dseq, nsel, pos0, n_pages, n_new, n_win):
    del pt_ref
    G, rep = NSA_KVH, NSA_REP
    rows = G * dseq * rep
    rpg = dseq * rep
    p_id = pl.program_id(1)
    row = lax.broadcasted_iota(jnp.int32, (rows, 1), 0)
    pos_rows = pos0 + ((row >> 3) & (dseq - 1))
    nbl = sel_scr.shape[1]
    groups = [(g * rpg, (g + 1) * rpg) for g in range(G)]

    @pl.when(p_id == 0)
    def _():
        s = jnp.concatenate([_dot_nt(qg_ref[g], ck_ref[g].astype(BF16)) for g in range(G)], axis=0)
        p, o_cmp = _cmp_branch(s, [cv_ref[g].astype(BF16) for g in range(G)], groups, pos_rows, nbl)
        ocmp_scr[...] = o_cmp
        imp = jnp.sum(p.reshape(rows // rep, rep, nbl), axis=1, keepdims=True)
        imp = jnp.broadcast_to(imp, (rows // rep, rep, nbl)).reshape(rows, nbl)
        sel_scr[...] = _select_blocks(imp, pos_rows, nsel)
        m_scr[...] = jnp.full(m_scr.shape, NEG_INF, F32)
        l_scr[...] = jnp.zeros(l_scr.shape, F32)
        acc_scr[...] = jnp.zeros(acc_scr.shape, F32)

    def scores(src_ref):
        return jnp.concatenate(
            [_dot_nt(qg_ref[g], _page_part(src_ref, 0, g).astype(BF16)) for g in range(G)], axis=0)

    def weighted_values(pb, src_ref):
        return jnp.concatenate(
            [_dot(pb[r0:r1], _page_part(src_ref, 1, g).astype(BF16)) for g, (r0, r1) in enumerate(groups)],
            axis=0)

    def attend(src_ref, bias):
        s = scores(src_ref) + bias
        m_old = m_scr[...]
        m_new = jnp.maximum(m_old, jnp.max(s, axis=1, keepdims=True))
        alpha = jnp.exp(m_old - m_new)
        pe = jnp.where(bias == 0.0, jnp.exp(s - m_new), 0.0)
        l_scr[...] = alpha * l_scr[...] + jnp.sum(pe, axis=1, keepdims=True)
        acc_scr[...] = alpha * acc_scr[...] + weighted_values(pe.astype(BF16), src_ref)
        m_scr[...] = m_new

    k0 = p_id * PAGE_SIZE
    key = k0 + lax.broadcasted_iota(jnp.int32, (1, PAGE_SIZE), 1)
    selexp = _expand_sel(sel_scr[...].astype(BF16), k0, PAGE_SIZE, nbl)
    attend(page_ref, jnp.where(jnp.where(key <= pos_rows, selexp, 0.0) > 0.5, 0.0, NEG_INF))

    @pl.when(p_id == n_pages - 1)
    def _():
        past = n_pages * PAGE_SIZE
        nk = new_ref.shape[0]
        idx = lax.broadcasted_iota(jnp.int32, (1, nk), 1)
        selexp_n = _expand_sel(sel_scr[...].astype(BF16), past, nk, nbl)
        ok = jnp.where(idx < n_new, jnp.where(past + idx <= pos_rows, selexp_n, 0.0), 0.0)
        attend(new_ref, jnp.where(ok > 0.5, 0.0, NEG_INF))
        o_slc = acc_scr[...] / l_scr[...]
        nw = win_ref.shape[0]
        widx = lax.broadcasted_iota(jnp.int32, (1, nw), 1)
        kpos = pos0 - (n_win - n_new) + widx
        okw = jnp.where(widx < n_win, 1.0, 0.0)
        okw = jnp.where(kpos >= 0, okw, 0.0)
        okw = jnp.where(kpos <= pos_rows, okw, 0.0)
        okw = jnp.where(kpos > pos_rows - WINDOW, okw, 0.0)
        sw = scores(win_ref) + jnp.where(okw > 0.5, 0.0, NEG_INF)
        pw = jnp.exp(sw - jnp.max(sw, axis=1, keepdims=True))
        o_win = weighted_values(pw.astype(BF16), win_ref) / jnp.sum(pw, axis=1, keepdims=True)
        gt = jax.nn.sigmoid(gate_ref[...])
        o_ref[...] = gt[:, 0:1] * ocmp_scr[...] + gt[:, 1:2] * o_slc + gt[:, 2:3] * o_win


def nsa_attn_sample(page_table, qg, ckv, cache_slc, new_rows, win_rows, gates_s,
                    *, dseq, nsel, pos0, n_new, n_win):
    bt, n_pages = page_table.shape
    rows = NSA_KVH * dseq * NSA_REP
    nbl = ckv.shape[3]
    assert dseq & (dseq - 1) == 0 and NSA_REP == 8
    page_tail = cache_slc.shape[1:]
    return pl.pallas_call(
        functools.partial(_nsa_sample_kernel, dseq=dseq, nsel=nsel, pos0=pos0, n_pages=n_pages,
                          n_new=n_new, n_win=n_win),
        grid_spec=pltpu.PrefetchScalarGridSpec(
            num_scalar_prefetch=1,
            grid=(bt, n_pages),
            in_specs=[
                pl.BlockSpec((None, NSA_KVH, dseq * NSA_REP, NSA_DH), lambda b, p, pt: (b, 0, 0, 0)),
                pl.BlockSpec((None, NSA_KVH, None, nbl, NSA_DH), lambda b, p, pt: (0, 0, b, 0, 0)),
                pl.BlockSpec((None, NSA_KVH, None, nbl, NSA_DH), lambda b, p, pt: (1, 0, b, 0, 0)),
                pl.BlockSpec((None,) + page_tail, lambda b, p, pt: (pt[b, p],) + (0,) * len(page_tail)),
                pl.BlockSpec((None,) + new_rows.shape[1:], lambda b, p, pt: (b, 0, 0)),
                pl.BlockSpec((None,) + win_rows.shape[1:], lambda b, p, pt: (b, 0, 0)),
                pl.BlockSpec((None, rows, LANES), lambda b, p, pt: (b, 0, 0)),
            ],
            out_specs=pl.BlockSpec((None, rows, NSA_DH), lambda b, p, pt: (b, 0, 0)),
            scratch_shapes=[pltpu.VMEM((rows, nbl), F32), pltpu.VMEM((rows, 1), F32),
                            pltpu.VMEM((rows, 1), F32), pltpu.VMEM((rows, NSA_DH), F32),
                            pltpu.VMEM((rows, NSA_DH), F32)],
        ),
        out_shape=jax.ShapeDtypeStruct((bt, rows, NSA_DH), F32),
        compiler_params=_cparams("parallel", "arbitrary"),
        name="nsa_attn_sample",
    )(page_table, qg, ckv, ckv, cache_slc, new_rows, win_rows, gates_s)


def _round_up(x, mult):
    return -(-x // mult) * mult


def nsa_mixer(xb, w, *, bt, pos0, page_table=None, cache_cmp=None, cache_slc=None, cache_win=None):
    m = xb.shape[0]
    L = m // bt
    G, R, dh = NSA_KVH, NSA_REP, NSA_DH
    proj = mm(xb, w['in_main'])
    gates = mm(xb, w['in_gate'], bn=LANES)[:, :3 * NSA_HEADS]
    cos, sin = _rope_tables(pos0 + jnp.arange(L), dh)
    if L % 8:
        cos, sin = jnp.tile(cos, (bt, 1)), jnp.tile(sin, (bt, 1))
    q, new_cmp, new_slc, new_win = nsa_prep(proj, cos, sin, rows_per_table=cos.shape[0])
    past = 0 if page_table is None else page_table.shape[1] * PAGE_SIZE
    total = past + L
    tp = _round_up(total, SEL_BLOCK)
    nsel = tp // SEL_BLOCK
    nbl = _round_up(tp // CMP_BLOCK, LANES)

    if page_table is None:
        src = new_cmp.reshape(m // PAGE_SIZE, PAGE_SIZE, NSA_ROW_W)
        pt = jnp.arange(m // PAGE_SIZE, dtype=jnp.int32).reshape(bt, L // PAGE_SIZE)
    else:
        src = cache_cmp
        pt = page_table
    xc = cmp_relayout(src, pt)
    nb = xc.shape[2] // bt
    ccos, csin = _rope_tables((jnp.arange(nb) + 1) * CMP_BLOCK - 1, dh)
    ccos = jnp.stack([jnp.tile(ccos, (bt, 1)), jnp.ones((bt * nb, dh), F32)])
    csin = jnp.stack([jnp.tile(csin, (bt, 1)), jnp.zeros((bt * nb, dh), F32)])
    ckv = compress(xc, w['pe'], w['c1'], w['c2'], ccos, csin).reshape(2, G, bt, nb, dh)
    ckv = jnp.pad(ckv, ((0, 0), (0, 0), (0, 0), (0, nbl - nb), (0, 0)))

    g5 = gates.reshape(bt, L, G, R, 3)
    if page_table is None:
        gates_g = jnp.transpose(g5, (2, 0, 1, 3, 4)).reshape(G, m, R * 3)
        gates_g = jnp.pad(gates_g, ((0, 0), (0, 0), (0, LANES - R * 3)))
        o = nsa_attn_prompt(q, ckv, new_slc, new_win, gates_g, bt=bt, nsel=nsel)
        win_keep = new_win.reshape(bt, L, NSA_ROW_W)[:, L - min(WINDOW, L):]
    else:
        q5 = q.reshape(bt, L, G, R, dh)
        qg = jnp.transpose(q5, (0, 2, 1, 3, 4)).reshape(bt, G, L * R, dh)
        gates_s = jnp.transpose(g5, (0, 2, 1, 3, 4)).reshape(bt, G * L * R, 3)
        gates_s = jnp.pad(gates_s, ((0, 0), (0, 0), (0, LANES - 3)))
        new3 = new_slc.reshape(bt, L, NSA_ROW_W)
        new_pad = jnp.pad(new3, ((0, 0), (0, _round_up(L, LANES) - L), (0, 0)))
        prefix = cache_win.reshape(bt, -1, NSA_ROW_W)
        win_all = jnp.concatenate([prefix, new_win.reshape(bt, L, NSA_ROW_W)], axis=1)
        n_win = win_all.shape[1]
        win_pad = jnp.pad(win_all, ((0, 0), (0, _round_up(n_win, LANES) - n_win), (0, 0)))
        o = nsa_attn_sample(pt, qg, ckv, cache_slc, new_pad, win_pad,
                            gates_s, dseq=L, nsel=nsel, pos0=pos0, n_new=L, n_win=n_win)
        o = jnp.transpose(o.reshape(bt, G, L, R, dh), (0, 2, 1, 3, 4)).reshape(m, NSA_HEADS * dh).astype(BF16)
        win_keep = win_all[:, n_win - prefix.shape[1]:]
    shape5 = (bt, L, 2, G, dh)
    return o, new_cmp.reshape(shape5), new_slc.reshape(shape5), win_keep.reshape(bt, -1, 2, G, dh)


def kernel(x_prompt, x_sample, state_ssm, state_ret, cache_cmp_kv, cache_slc_kv, cache_win_kv, state_pool, page_table, ln_g, ln_b, ffn_w_gate, ffn_w_up, ffn_w_down, ssm_a_re, ssm_a_im, ssm_log_dt, ssm_b_re, ssm_b_im, ssm_c_re, ssm_c_im, ssm_d, ssm_w_glu, ret_w_in, ret_norm_g, ret_w_out, nsa_w_in, nsa_pe_ck, nsa_w_ck1, nsa_w_ck2, nsa_pe_cv, nsa_w_cv1, nsa_w_cv2, nsa_w_out, pool_w, pool_scale):
    past_len = page_table.shape[1] * PAGE_SIZE

    wg = ffn_w_gate
    wu = ffn_w_up
    wd = ffn_w_down.astype(BF16)
    w_glu = ssm_w_glu[0]
    r_w_in = ret_w_in[0]
    r_w_out = ret_w_out[0].astype(BF16)
    n_main = NSA_HEADS * NSA_DH + 6 * NSA_KV_W
    nsa_w = {
        'in_main': nsa_w_in[0, :, :n_main].astype(BF16),
        'in_gate': jnp.pad(nsa_w_in[0, :, n_main:], ((0, 0), (0, LANES - 3 * NSA_HEADS))).astype(BF16),
        'pe': jnp.stack([nsa_pe_ck[0], nsa_pe_cv[0]]).astype(F32),
        'c1': jnp.stack([nsa_w_ck1[0], nsa_w_cv1[0]]).astype(BF16),
        'c2': jnp.stack([nsa_w_ck2[0], nsa_w_cv2[0]]).astype(BF16),
    }
    n_w_out = nsa_w_out[0].astype(BF16)
    p_w = pool_w[0].astype(BF16)
    a_tab, bblk, cblk = _ssm_tables(ssm_a_re[0], ssm_a_im[0], ssm_log_dt[0], ssm_b_re[0], ssm_b_im[0],
                                    ssm_c_re[0], ssm_c_im[0])

    def ffn(xf, xb, i, s, ln_idx):
        h = dual_mm(xb, wg, wu, combine=_swiglu_combine, n=D_FF, widx_a=(i, s), widx_b=(i, s), bn=256)
        return mm_res_ln(h, wd, xf, ln_g[i, ln_idx], ln_b[i, ln_idx], widx=(i, s), scale=0.5)

    def stream(x, pos0, is_prompt):
        bt, L, _ = x.shape
        m = bt * L
        xf = x.astype(F32).reshape(m, D_MODEL)
        xb = xf.astype(BF16)
        outs = {}
        for i in range(DEPTH):
            mixer = i % N_MIXERS
            xf, xb = ffn(xf, xb, i, 0, 0)
            if mixer == 0:
                h0 = jnp.zeros((bt, SSM_GROUPS, SSM_N, 2), F32) if is_prompt else state_ssm[0]
                yg, outs['ssm'] = ssm_scan(xf, h0, a_tab, bblk, cblk, ssm_d[0], bt=bt)
                y = dual_mm(yg, w_glu, w_glu, combine=_glu_combine, n=D_MODEL, col0_b=D_MODEL,
                            out_dtype=BF16, bn=256)
                xf, xb = res_ln(y, xf, ln_g[i, 1], ln_b[i, 1])
            else:
                if mixer == 1:
                    proj = mm(xb, r_w_in)
                    s0 = None if is_prompt else state_ret[0]
                    o, outs['ret'] = retention_core(proj, s0, ret_norm_g[0], bt=bt, pos0=pos0)
                    w_o = r_w_out
                elif mixer == 2:
                    if is_prompt:
                        o, c_new, s_new, w_new = nsa_mixer(xb, nsa_w, bt=bt, pos0=pos0)
                    else:
                        o, c_new, s_new, w_new = nsa_mixer(
                            xb, nsa_w, bt=bt, pos0=pos0, page_table=page_table,
                            cache_cmp=cache_cmp_kv[0], cache_slc=cache_slc_kv[0], cache_win=cache_win_kv[0])
                    outs['cmp'], outs['slc'], outs['win'] = c_new, s_new, w_new
                    w_o = n_w_out
                else:
                    prefix = jnp.zeros((bt, POOL_PAST, D_MODEL), F32) if is_prompt else state_pool[0]
                    y = pool_core(xf, prefix, p_w, pool_scale[0], bt=bt, pos0=pos0)
                    x3 = xf.reshape(bt, L, D_MODEL)
                    if L < POOL_PAST:
                        x3 = jnp.concatenate([prefix.astype(F32)[:, L:], x3], axis=1)
                    outs['pool'] = x3[:, -POOL_PAST:]
                if mixer == 3:
                    xf, xb = res_ln(y, xf, ln_g[i, 1], ln_b[i, 1])
                else:
                    xf, xb = mm_res_ln(o, w_o, xf, ln_g[i, 1], ln_b[i, 1])
            xf, xb = ffn(xf, xb, i, 1, 2)
        return xf.reshape(bt, L, D_MODEL), outs

    yp, op = stream(x_prompt, 0, True)
    ys, os_ = stream(x_sample, past_len, False)
    return (yp.astype(x_prompt.dtype), ys.astype(x_sample.dtype),
            op['ssm'][None], os_['ssm'][None], op['ret'][None], os_['ret'][None],
            op['cmp'][None], os_['cmp'][None], op['slc'][None], os_['slc'][None],
            op['win'][None], os_['win'][None], op['pool'][None], os_['pool'][None])
```

```python
import functools
import math

import jax
import jax.numpy as jnp
from jax import lax
from jax.experimental import pallas as pl
from jax.experimental.pallas import tpu as pltpu

F32 = jnp.float32
BF16 = jnp.bfloat16

D_MODEL = 4096
DEPTH = 4
PAGE_SIZE = 128
N_MIXERS = 4
D_FF = 2 * D_MODEL
DN_ALPHA = (2.0 * DEPTH) ** 0.25
LN_EPS = 1e-5
ROPE_THETA = 10000.0
NEG_INF = -1e30

SSM_GROUP = 16
SSM_GROUPS = D_MODEL // SSM_GROUP
SSM_N = 64
SSM_TILE_GROUPS = 16
SSM_TILE_CH = SSM_TILE_GROUPS * SSM_GROUP
SSM_TILE_ST = SSM_TILE_GROUPS * SSM_N

RET_DK = 256
RET_HEADS = D_MODEL // RET_DK
RET_DV = 2 * RET_DK
RET_CHUNK = 128
RET_HEADS_PER_STEP = 2

NSA_DH = 128
NSA_HEADS = D_MODEL // NSA_DH
NSA_KVH = 4
NSA_REP = NSA_HEADS // NSA_KVH
CMP_BLOCK = 32
SEL_BLOCK = 64
N_SEL = 16
WINDOW = 512
SEL_FORCE = 1e4
NSA_KV_W = NSA_KVH * NSA_DH
NSA_ROW_W = 2 * NSA_KV_W
NSA_TQ = 128

POOL_WINDOWS = (2, 4, 8, 16)
POOL_CH = D_MODEL // len(POOL_WINDOWS)
POOL_PAST = max(POOL_WINDOWS) - 1
POOL_HALO = 16

LANES = 128
V7X_VMEM_BYTES = 64 * 1024 * 1024
VMEM_LIMIT = V7X_VMEM_BYTES - 8 * 1024 * 1024


def _cparams(*sem):
    return pltpu.CompilerParams(dimension_semantics=sem, vmem_limit_bytes=VMEM_LIMIT)


def _dot(a, b):
    return jnp.dot(a, b, preferred_element_type=F32)


def _dot_nt(a, b):
    return lax.dot_general(a, b, (((1,), (1,)), ((), ())), preferred_element_type=F32)


def _dot_tn(a, b):
    return lax.dot_general(a, b, (((0,), (0,)), ((), ())), preferred_element_type=F32)


def _mm_kernel(x_ref, w_ref, o_ref):
    o_ref[...] = _dot(x_ref[...], w_ref[...].astype(BF16)).astype(o_ref.dtype)


def _w_spec(k, bn, col0_blocks, widx):
    lead = tuple(widx)
    shape = (None,) * len(lead) + (k, bn)
    return pl.BlockSpec(shape, lambda i, j: lead + (0, j + col0_blocks))


def mm(x, w, *, widx=(), n=None, col0=0, out_dtype=F32, bm=None, bn=512):
    m, k = x.shape
    n = w.shape[-1] if n is None else n
    if bm is None:
        bm = min(m, 1024 if k <= 4096 else 512)
    bn = min(bn, n)
    assert m % bm == 0 and n % bn == 0 and col0 % bn == 0
    return pl.pallas_call(
        _mm_kernel,
        grid=(m // bm, n // bn),
        in_specs=[pl.BlockSpec((bm, k), lambda i, j: (i, 0)),
                  _w_spec(k, bn, col0 // bn, widx)],
        out_specs=pl.BlockSpec((bm, bn), lambda i, j: (i, j)),
        out_shape=jax.ShapeDtypeStruct((m, n), out_dtype),
        compiler_params=_cparams("parallel", "arbitrary"),
        name="mm",
    )(x, w)


def _dual_mm_kernel(x_ref, wa_ref, wb_ref, o_ref, *, combine):
    x = x_ref[...]
    a = _dot(x, wa_ref[...].astype(BF16))
    b = _dot(x, wb_ref[...].astype(BF16))
    o_ref[...] = combine(a, b).astype(o_ref.dtype)


def _swiglu_combine(a, b):
    return a * jax.nn.sigmoid(a) * b


def _glu_combine(a, b):
    return a * jax.nn.sigmoid(b)


def dual_mm(x, wa, wb, *, combine, n, widx_a=(), widx_b=(), col0_a=0, col0_b=0,
            out_dtype=BF16, bm=None, bn=512):
    m, k = x.shape
    if bm is None:
        bm = min(m, 1024)
    assert m % bm == 0 and n % bn == 0 and col0_a % bn == 0 and col0_b % bn == 0
    return pl.pallas_call(
        functools.partial(_dual_mm_kernel, combine=combine),
        grid=(m // bm, n // bn),
        in_specs=[pl.BlockSpec((bm, k), lambda i, j: (i, 0)),
                  _w_spec(k, bn, col0_a // bn, widx_a),
                  _w_spec(k, bn, col0_b // bn, widx_b)],
        out_specs=pl.BlockSpec((bm, bn), lambda i, j: (i, j)),
        out_shape=jax.ShapeDtypeStruct((m, n), out_dtype),
        compiler_params=_cparams("parallel", "arbitrary"),
        name="dual_mm",
    )(x, wa, wb)


def _res_ln_kernel(y_ref, r_ref, g_ref, b_ref, of_ref, ob_ref, *, scale):
    z = DN_ALPHA * r_ref[...] + scale * y_ref[...].astype(F32)
    mu = jnp.mean(z, axis=-1, keepdims=True)
    zc = z - mu
    var = jnp.mean(zc * zc, axis=-1, keepdims=True)
    out = zc * lax.rsqrt(var + LN_EPS) * g_ref[...] + b_ref[...]
    of_ref[...] = out
    ob_ref[...] = out.astype(BF16)


def res_ln(y, resid, g, b, *, scale=1.0, bm=256):
    m, d = resid.shape
    bm = min(bm, m)
    assert m % bm == 0
    row = pl.BlockSpec((bm, d), lambda i: (i, 0))
    vec = pl.BlockSpec((1, d), lambda i: (0, 0))
    return pl.pallas_call(
        functools.partial(_res_ln_kernel, scale=scale),
        grid=(m // bm,),
        in_specs=[row, row, vec, vec],
        out_specs=[row, row],
        out_shape=[jax.ShapeDtypeStruct((m, d), F32), jax.ShapeDtypeStruct((m, d), BF16)],
        compiler_params=_cparams("parallel"),
        name="res_ln",
    )(y, resid, g.reshape(1, d), b.reshape(1, d))


def mm_res_ln(h, w, resid, g, b, *, widx=(), scale=1.0):
    return res_ln(mm(h, w, widx=widx, out_dtype=BF16), resid, g, b, scale=scale)


def _ssm_kernel(u_ref, h0_ref, a_ref, bblk_ref, cblk_ref, d_ref, y_ref, hout_ref,
                h_scr, u_scr, y_scr, bu_scr, hs_scr, *, bt, tc, nchunks):
    c = pl.program_id(1)
    st = SSM_TILE_ST
    nl = SSM_TILE_CH // LANES

    @pl.when(c == 0)
    def _():
        h_scr[...] = jnp.zeros_like(h_scr)
        h_scr[0:bt, :] = h0_ref[...]

    for b in range(bt):
        for j in range(nl):
            u_scr[j, pl.ds(b, tc, stride=bt), :] = u_ref[b, :, j * LANES:(j + 1) * LANES]
    u = jnp.concatenate([u_scr[j] for j in range(nl)], axis=1)
    bu_scr[...] = _dot(u.astype(BF16), bblk_ref[...])
    ar = a_ref[0:1, :]
    ai = a_ref[1:2, :]

    def advance(hr, hi, sr, si):
        return ar * hr - ai * hi + sr, ar * hi + ai * hr + si

    if bt == 8:
        def step(t, carry):
            r0 = pl.multiple_of(t * 8, 8)
            hr, hi = advance(*carry, bu_scr[pl.ds(r0, 8), 0:st], bu_scr[pl.ds(r0, 8), st:2 * st])
            hs_scr[pl.ds(r0, 8), 0:st] = hr
            hs_scr[pl.ds(r0, 8), st:2 * st] = hi
            return hr, hi
        n_it = tc
    else:
        low = lax.broadcasted_iota(jnp.int32, (8, st), 0) < bt

        def step(t2, carry):
            r0 = pl.multiple_of(t2 * 8, 8)
            sr = bu_scr[pl.ds(r0, 8), 0:st]
            si = bu_scr[pl.ds(r0, 8), st:2 * st]
            e_r, e_i = advance(*carry, sr, si)
            o_r, o_i = advance(pltpu.roll(e_r, bt, 0), pltpu.roll(e_i, bt, 0), sr, si)
            hs_scr[pl.ds(r0, 8), 0:st] = jnp.where(low, e_r, o_r)
            hs_scr[pl.ds(r0, 8), st:2 * st] = jnp.where(low, e_i, o_i)
            return pltpu.roll(o_r, bt, 0), pltpu.roll(o_i, bt, 0)
        n_it = tc // 2

    hr, hi = lax.fori_loop(0, n_it, step, (h_scr[:, 0:st], h_scr[:, st:2 * st]))
    h_scr[:, 0:st] = hr
    h_scr[:, st:2 * st] = hi
    y = jax.nn.gelu(_dot(hs_scr[...].astype(BF16), cblk_ref[...]) + u * d_ref[...])
    for j in range(nl):
        y_scr[j] = y[:, j * LANES:(j + 1) * LANES]
    for b in range(bt):
        for j in range(nl):
            y_ref[b, :, j * LANES:(j + 1) * LANES] = y_scr[j, pl.ds(b, tc, stride=bt), :].astype(y_ref.dtype)

    @pl.when(c == nchunks - 1)
    def _():
        hout_ref[...] = h_scr[0:bt, :]


def _ssm_tables(a_re, a_im, log_dt, b_re, b_im, c_re, c_im):
    ar = a_re.astype(F32)
    ai = a_im.astype(F32)
    dt = jnp.exp(log_dt.astype(F32))[:, None]
    mag = jnp.exp(ar * dt)
    abar_r = mag * jnp.cos(ai * dt)
    abar_i = mag * jnp.sin(ai * dt)
    den = ar * ar + ai * ai
    zr = abar_r - 1.0
    coef_r = (zr * ar + abar_i * ai) / den
    coef_i = (abar_i * ar - zr * ai) / den
    br = b_re.astype(F32)
    bi = b_im.astype(F32)
    bbar_r = coef_r[..., None] * br - coef_i[..., None] * bi
    bbar_i = coef_r[..., None] * bi + coef_i[..., None] * br
    gt = SSM_TILE_GROUPS
    nt = SSM_GROUPS // gt
    eye = jnp.eye(gt, dtype=F32)
    bb = jnp.stack([bbar_r, bbar_i]).reshape(2, nt, gt, SSM_N, SSM_GROUP)
    bblk = jnp.einsum('rtgnc,gh->tgcrhn', bb, eye).reshape(nt, SSM_TILE_CH, 2 * SSM_TILE_ST)
    cc = jnp.stack([c_re.astype(F32), -c_im.astype(F32)]).reshape(2, nt, gt, SSM_GROUP, SSM_N)
    cblk = jnp.einsum('rtgcn,gh->trgnhc', cc, eye).reshape(nt, 2 * SSM_TILE_ST, SSM_TILE_CH)
    a_tab = jnp.stack([abar_r.reshape(nt, SSM_TILE_ST), abar_i.reshape(nt, SSM_TILE_ST)], axis=1)
    return a_tab, bblk.astype(BF16), cblk.astype(BF16)


def ssm_scan(x, h0, a_tab, bblk, cblk, d_skip, *, bt):
    m, d = x.shape
    L = m // bt
    assert bt in (4, 8)
    nt = SSM_GROUPS // SSM_TILE_GROUPS
    tc = min(L, 256)
    assert L % tc == 0 and (tc * bt) % 16 == 0
    nchunks = L // tc
    rows = tc * bt
    h0t = h0.astype(F32).reshape(bt, nt, SSM_TILE_GROUPS, SSM_N, 2)
    h0t = jnp.transpose(h0t, (1, 0, 4, 2, 3)).reshape(nt, bt, 2 * SSM_TILE_ST)
    tok = pl.BlockSpec((bt, tc, SSM_TILE_CH), lambda j, c: (0, c, j))
    y, hout = pl.pallas_call(
        functools.partial(_ssm_kernel, bt=bt, tc=tc, nchunks=nchunks),
        grid=(nt, nchunks),
        in_specs=[
            tok,
            pl.BlockSpec((None, bt, 2 * SSM_TILE_ST), lambda j, c: (j, 0, 0)),
            pl.BlockSpec((None, 2, SSM_TILE_ST), lambda j, c: (j, 0, 0)),
            pl.BlockSpec((None, SSM_TILE_CH, 2 * SSM_TILE_ST), lambda j, c: (j, 0, 0)),
            pl.BlockSpec((None, 2 * SSM_TILE_ST, SSM_TILE_CH), lambda j, c: (j, 0, 0)),
            pl.BlockSpec((1, SSM_TILE_CH), lambda j, c: (0, j)),
        ],
        out_specs=[tok, pl.BlockSpec((None, bt, 2 * SSM_TILE_ST), lambda j, c: (j, 0, 0))],
        out_shape=[jax.ShapeDtypeStruct((bt, L, d), BF16),
                   jax.ShapeDtypeStruct((nt, bt, 2 * SSM_TILE_ST), F32)],
        scratch_shapes=[pltpu.VMEM((8, 2 * SSM_TILE_ST), F32),
                        pltpu.VMEM((SSM_TILE_CH // LANES, rows, LANES), F32),
                        pltpu.VMEM((SSM_TILE_CH // LANES, rows, LANES), F32),
                        pltpu.VMEM((rows, 2 * SSM_TILE_ST), F32),
                        pltpu.VMEM((rows, 2 * SSM_TILE_ST), F32)],
        compiler_params=_cparams("parallel", "arbitrary"),
        name="ssm_scan",
    )(x.reshape(bt, L, d), h0t, a_tab, bblk, cblk, d_skip.astype(F32).reshape(1, d))
    hfin = hout.reshape(nt, bt, 2, SSM_TILE_GROUPS, SSM_N)
    hfin = jnp.transpose(hfin, (1, 0, 3, 4, 2)).reshape(bt, SSM_GROUPS, SSM_N, 2)
    return y.reshape(m, d), hfin


def _ret_kernel(*refs, nc, zero_init):
    if zero_init:
        (q_ref, k_ref, v_ref, g_ref, cos_ref, sin_ref, intra_ref, qd_ref, kd_ref, cd_ref, ng_ref,
         o_ref, sfin_ref, s_scr) = refs
        s0_ref = None
    else:
        (q_ref, k_ref, v_ref, g_ref, cos_ref, sin_ref, intra_ref, qd_ref, kd_ref, cd_ref, ng_ref,
         s0_ref, o_ref, sfin_ref, s_scr) = refs
    c = pl.program_id(2)

    @pl.when(c == 0)
    def _():
        if zero_init:
            s_scr[...] = jnp.zeros_like(s_scr)
        else:
            s_scr[...] = s0_ref[...]

    cos = cos_ref[...]
    sin = sin_ref[...]
    half = RET_DK // 2

    def rope(x):
        x1 = x[:, :half]
        x2 = x[:, half:]
        return jnp.concatenate([x1 * cos - x2 * sin, x2 * cos + x1 * sin], axis=1)

    for hh in range(s_scr.shape[0]):
        qk = slice(hh * RET_DK, (hh + 1) * RET_DK)
        vg = slice(hh * RET_DV, (hh + 1) * RET_DV)
        q = rope(q_ref[:, qk].astype(F32))
        k = rope(k_ref[:, qk].astype(F32)) * RET_DK ** -0.5
        qb = q.astype(BF16)
        vb = v_ref[:, vg].astype(BF16)
        sc = _dot_nt(qb, k.astype(BF16)) * intra_ref[hh]
        s_old = s_scr[hh]
        o = _dot(sc.astype(BF16), vb) + _dot(qb, s_old.astype(BF16)) * qd_ref[hh]
        s_new = s_old * cd_ref[hh] + _dot_tn((k * kd_ref[hh]).astype(BF16), vb)
        s_scr[hh] = s_new
        mu = jnp.mean(o, axis=-1, keepdims=True)
        oc = o - mu
        var = jnp.mean(oc * oc, axis=-1, keepdims=True)
        on = oc * lax.rsqrt(var + LN_EPS) * ng_ref[:, vg]
        g = g_ref[:, vg].astype(F32)
        o_ref[:, vg] = (g * jax.nn.sigmoid(g) * on).astype(o_ref.dtype)

    @pl.when(c == nc - 1)
    def _():
        sfin_ref[...] = s_scr[...]


def retention_core(proj, s0, norm_g, *, bt, pos0):
    m, n = proj.shape
    L = m // bt
    H = RET_HEADS
    C = math.gcd(L, RET_CHUNK)
    nc = L // C
    pos = (pos0 + jnp.arange(L)).astype(F32)
    inv = 1.0 / (ROPE_THETA ** jnp.linspace(0.0, 1.0, RET_DK // 2, dtype=F32))
    ang = pos[:, None] * inv[None, :]
    cos_t, sin_t = jnp.cos(ang), jnp.sin(ang)
    log_g = jnp.log1p(-jnp.exp2(-5.0 - jnp.arange(H, dtype=F32)))
    idx = jnp.arange(C, dtype=F32)
    diff = idx[:, None] - idx[None, :]
    intra = jnp.where(diff >= 0, jnp.exp(log_g[:, None, None] * jnp.maximum(diff, 0.0)), 0.0)
    q_dec = jnp.exp(log_g[:, None] * (idx[None, :] + 1.0))[..., None]
    k_dec = jnp.exp(log_g[:, None] * (C - 1.0 - idx[None, :]))[..., None]
    chunk_dec = jnp.exp(log_g * C).reshape(H, 1, 1)
    zero_init = s0 is None
    proj3 = proj.reshape(bt, L, n)
    kb0 = H
    vb0 = 2 * H * RET_DK // RET_DV
    gb0 = vb0 + H
    hb = RET_HEADS_PER_STEP
    assert H % hb == 0
    in_specs = [
        pl.BlockSpec((None, C, hb * RET_DK), lambda b, h, c: (b, c, h)),
        pl.BlockSpec((None, C, hb * RET_DK), lambda b, h, c: (b, c, kb0 // hb + h)),
        pl.BlockSpec((None, C, hb * RET_DV), lambda b, h, c: (b, c, vb0 // hb + h)),
        pl.BlockSpec((None, C, hb * RET_DV), lambda b, h, c: (b, c, gb0 // hb + h)),
        pl.BlockSpec((C, RET_DK // 2), lambda b, h, c: (c, 0)),
        pl.BlockSpec((C, RET_DK // 2), lambda b, h, c: (c, 0)),
        pl.BlockSpec((hb, C, C), lambda b, h, c: (h, 0, 0)),
        pl.BlockSpec((hb, C, 1), lambda b, h, c: (h, 0, 0)),
        pl.BlockSpec((hb, C, 1), lambda b, h, c: (h, 0, 0)),
        pl.BlockSpec((hb, 1, 1), lambda b, h, c: (h, 0, 0)),
        pl.BlockSpec((1, hb * RET_DV), lambda b, h, c: (0, h)),
    ]
    args = [proj3, proj3, proj3, proj3, cos_t, sin_t, intra, q_dec, k_dec, chunk_dec,
            norm_g.astype(F32).reshape(1, H * RET_DV)]
    st_spec = pl.BlockSpec((None, hb, RET_DK, RET_DV), lambda b, h, c: (b, h, 0, 0))
    if not zero_init:
        in_specs.append(st_spec)
        args.append(s0.astype(F32))
    o, sfin = pl.pallas_call(
        functools.partial(_ret_kernel, nc=nc, zero_init=zero_init),
        grid=(bt, H // hb, nc),
        in_specs=in_specs,
        out_specs=[pl.BlockSpec((None, C, hb * RET_DV), lambda b, h, c: (b, c, h)), st_spec],
        out_shape=[jax.ShapeDtypeStruct((bt, L, H * RET_DV), BF16),
                   jax.ShapeDtypeStruct((bt, H, RET_DK, RET_DV), F32)],
        scratch_shapes=[pltpu.VMEM((hb, RET_DK, RET_DV), F32)],
        compiler_params=_cparams("parallel", "parallel", "arbitrary"),
        name="retention_core",
    )(*args)
    return o.reshape(m, H * RET_DV), sfin


def _pool_kernel(*refs, tile, pos0, has_halo):
    if has_halo:
        x_ref, halo_ref, pre_ref, w_ref, sc_ref, y_ref, xx_scr = refs
    else:
        x_ref, pre_ref, w_ref, sc_ref, y_ref, xx_scr = refs
        halo_ref = None
    i = pl.program_id(1)
    hl = POOL_HALO
    if has_halo:
        @pl.when(i == 0)
        def _():
            xx_scr[0:hl, :] = pre_ref[...]

        @pl.when(i > 0)
        def _():
            xx_scr[0:hl, :] = halo_ref[...]
    else:
        xx_scr[0:hl, :] = pre_ref[...]
    xx_scr[hl:hl + tile, :] = x_ref[...]
    pos = pos0 + i * tile + lax.broadcasted_iota(jnp.int32, (tile, 1), 0)
    for gi, w in enumerate(POOL_WINDOWS):
        c0, c1 = gi * POOL_CH, (gi + 1) * POOL_CH
        xs = xx_scr[hl:hl + tile, c0:c1]
        acc = xs
        for k in range(1, w):
            acc = acc + xx_scr[hl - k:hl - k + tile, c0:c1]
        cnt = jnp.minimum(w, pos + 1).astype(F32)
        dlt = (acc / cnt - xs).astype(BF16)
        y_ref[:, c0:c1] = _dot(dlt, w_ref[gi]) * sc_ref[:, c0:c1]


def pool_core(xf, prefix, w_pool, scale, *, bt, pos0):
    m, d = xf.shape
    L = m // bt
    tile = min(L, 256)
    assert L % tile == 0
    has_halo = L > tile
    x3 = xf.reshape(bt, L, d)
    pre = jnp.pad(prefix.astype(F32), ((0, 0), (POOL_HALO - POOL_PAST, 0), (0, 0)))
    hb = tile // POOL_HALO
    row = pl.BlockSpec((None, tile, d), lambda b, i: (b, i, 0))
    in_specs = [row]
    args = [x3]
    if has_halo:
        in_specs.append(pl.BlockSpec((None, POOL_HALO, d), lambda b, i: (b, jnp.maximum(i * hb - 1, 0), 0)))
        args.append(x3)
    in_specs += [
        pl.BlockSpec((None, POOL_HALO, d), lambda b, i: (b, 0, 0)),
        pl.BlockSpec((len(POOL_WINDOWS), POOL_CH, POOL_CH), lambda b, i: (0, 0, 0)),
        pl.BlockSpec((1, d), lambda b, i: (0, 0)),
    ]
    args += [pre, w_pool, scale.astype(F32).reshape(1, d)]
    y = pl.pallas_call(
        functools.partial(_pool_kernel, tile=tile, pos0=pos0, has_halo=has_halo),
        grid=(bt, L // tile),
        in_specs=in_specs,
        out_specs=row,
        out_shape=jax.ShapeDtypeStruct((bt, L, d), F32),
        scratch_shapes=[pltpu.VMEM((POOL_HALO + tile, d), F32)],
        compiler_params=_cparams("parallel", "arbitrary"),
        name="pool_core",
    )(*args)
    return y.reshape(m, d)


def _rope_tables(pos, dim):
    inv = 1.0 / (ROPE_THETA ** (jnp.arange(0, dim, 2, dtype=F32) / dim))
    ang = pos.astype(F32)[:, None] * inv[None, :]
    cos = jnp.cos(ang)
    sin = jnp.sin(ang)
    return jnp.concatenate([cos, cos], axis=1), jnp.concatenate([-sin, sin], axis=1)


def _rope128(x, cos, sin):
    return x * cos + pltpu.roll(x, NSA_DH // 2, 1) * sin


def _nsa_prep_kernel(p_ref, cos_ref, sin_ref, q_ref, cmp_ref, slc_ref, win_ref):
    cos = cos_ref[...]
    sin = sin_ref[...]
    dh = NSA_DH
    qw = NSA_HEADS * dh
    for h in range(NSA_HEADS):
        x = p_ref[:, h * dh:(h + 1) * dh]
        q_ref[:, h * dh:(h + 1) * dh] = (_rope128(x, cos, sin) * dh ** -0.5).astype(q_ref.dtype)
    kvw = NSA_KV_W
    cmp_ref[...] = p_ref[:, qw:qw + 2 * kvw]
    for dst, base in ((slc_ref, qw + 2 * kvw), (win_ref, qw + 4 * kvw)):
        for g in range(NSA_KVH):
            dst[:, g * dh:(g + 1) * dh] = _rope128(p_ref[:, base + g * dh:base + (g + 1) * dh], cos, sin)
        dst[:, kvw:2 * kvw] = p_ref[:, base + kvw:base + 2 * kvw]


def nsa_prep(proj, cos, sin, *, rows_per_table):
    m, n = proj.shape
    bm = min(m, 256)
    assert m % bm == 0 and rows_per_table % bm == 0
    nt = rows_per_table // bm
    row = lambda w: pl.BlockSpec((bm, w), lambda i: (i, 0))
    tab = pl.BlockSpec((bm, NSA_DH), lambda i: (i % nt, 0))
    return pl.pallas_call(
        _nsa_prep_kernel,
        grid=(m // bm,),
        in_specs=[row(n), tab, tab],
        out_specs=[row(NSA_HEADS * NSA_DH), row(NSA_ROW_W), row(NSA_ROW_W), row(NSA_ROW_W)],
        out_shape=[jax.ShapeDtypeStruct((m, NSA_HEADS * NSA_DH), BF16)]
        + [jax.ShapeDtypeStruct((m, NSA_ROW_W), F32)] * 3,
        compiler_params=_cparams("parallel"),
        name="nsa_prep",
    )(proj, cos, sin)


def _page_part(page_ref, kv, g):
    if len(page_ref.shape) == 2:
        c0 = kv * NSA_KV_W + g * NSA_DH
        return page_ref[:, c0:c0 + NSA_DH]
    return page_ref[:, kv, g, :]


def _cmp_relayout_kernel(pt_ref, src_ref, out_ref):
    del pt_ref
    per_page = PAGE_SIZE // CMP_BLOCK
    for kv in range(2):
        for g in range(NSA_KVH):
            out_ref[kv, g] = _page_part(src_ref, kv, g).reshape(per_page, CMP_BLOCK, NSA_DH)


def cmp_relayout(src_pages, page_table):
    bt, n_pages = page_table.shape
    per_page = PAGE_SIZE // CMP_BLOCK
    nb = n_pages * per_page
    tail = src_pages.shape[1:]
    zeros = (0,) * len(tail)
    out = pl.pallas_call(
        _cmp_relayout_kernel,
        grid_spec=pltpu.PrefetchScalarGridSpec(
            num_scalar_prefetch=1,
            grid=(bt, n_pages),
            in_specs=[pl.BlockSpec((None,) + tail, lambda b, p, pt: (pt[b, p],) + zeros)],
            out_specs=pl.BlockSpec((2, NSA_KVH, None, per_page, CMP_BLOCK, NSA_DH),
                                   lambda b, p, pt: (0, 0, b, p, 0, 0)),
        ),
        out_shape=jax.ShapeDtypeStruct((2, NSA_KVH, bt, nb, CMP_BLOCK, NSA_DH), F32),
        compiler_params=_cparams("parallel", "parallel"),
        name="cmp_relayout",
    )(page_table, src_pages)
    return out.reshape(2, NSA_KVH, bt * nb, CMP_BLOCK, NSA_DH)


def _compress_kernel(x_ref, pe_ref, w1_ref, w2_ref, cos_ref, sin_ref, o_ref):
    dh = NSA_DH
    acc = jnp.zeros(o_ref.shape, F32)
    for r in range(CMP_BLOCK):
        h = (x_ref[:, r, :] + pe_ref[r:r + 1, :]).astype(BF16)
        acc = acc + _dot(h, w1_ref[r * dh:(r + 1) * dh, :])
    y = _dot(jax.nn.gelu(acc).astype(BF16), w2_ref[...])
    o_ref[...] = _rope128(y, cos_ref[...], sin_ref[...])


def compress(xc, pe, w1, w2, cos, sin):
    _, g, r, _, _ = xc.shape
    kdim = CMP_BLOCK * NSA_DH
    bm = min(r, 512)
    assert r % bm == 0
    return pl.pallas_call(
        _compress_kernel,
        grid=(2, g, r // bm),
        in_specs=[
            pl.BlockSpec((None, None, bm, CMP_BLOCK, NSA_DH), lambda kv, gi, i: (kv, gi, i, 0, 0)),
            pl.BlockSpec((None, CMP_BLOCK, NSA_DH), lambda kv, gi, i: (kv, 0, 0)),
            pl.BlockSpec((None, kdim, NSA_DH), lambda kv, gi, i: (kv, 0, 0)),
            pl.BlockSpec((None, NSA_DH, NSA_DH), lambda kv, gi, i: (kv, 0, 0)),
            pl.BlockSpec((None, bm, NSA_DH), lambda kv, gi, i: (kv, i, 0)),
            pl.BlockSpec((None, bm, NSA_DH), lambda kv, gi, i: (kv, i, 0)),
        ],
        out_specs=pl.BlockSpec((None, None, bm, NSA_DH), lambda kv, gi, i: (kv, gi, i, 0)),
        out_shape=jax.ShapeDtypeStruct((2, g, r, NSA_DH), F32),
        compiler_params=_cparams("parallel", "parallel", "parallel"),
        name="compress",
    )(xc, pe, w1, w2, cos, sin)


def _cmp_branch(s, cv_list, row_groups, pos_rows, nbl):
    blk_end = (lax.broadcasted_iota(jnp.int32, (1, nbl), 1) + 1) * CMP_BLOCK - 1
    m_cmp = blk_end <= pos_rows
    s = jnp.where(m_cmp, s, NEG_INF)
    mx = jnp.max(s, axis=1, keepdims=True)
    e = jnp.where(m_cmp, jnp.exp(s - mx), 0.0)
    den = jnp.sum(e, axis=1, keepdims=True)
    p = e / jnp.where(den > 0.0, den, 1.0)
    pb = p.astype(BF16)
    outs = [_dot(pb[r0:r1], cv) for (r0, r1), cv in zip(row_groups, cv_list)]
    o = outs[0] if len(outs) == 1 else jnp.concatenate(outs, axis=0)
    return p, o


def _select_blocks(imp, pos_l, nsel):
    rows, nbl = imp.shape
    imp2 = imp + pltpu.roll(imp, nbl - 1, 1)
    lane = lax.broadcasted_iota(jnp.int32, (rows, nbl), 1)
    j = lane >> 1
    valid = jnp.where((lane & 1) == 0, j, nsel) < nsel
    cur = pos_l >> 6
    forced = jnp.where(j == 0, 1, jnp.where(j == cur, 1, jnp.where(j == cur - 1, 1, 0))) > 0
    future = j * SEL_BLOCK > pos_l
    score = jnp.where(forced, SEL_FORCE, jnp.where(future, -SEL_FORCE, imp2))
    score = jnp.where(valid, score, -3.0e38)
    rank = jnp.zeros((rows, nbl), F32)
    for i in range(nsel):
        col = score[:, 2 * i:2 * i + 1]
        tie = jnp.where(lane > 2 * i, 1.0, 0.0)
        rank = rank + jnp.where(col > score, 1.0, jnp.where(col == score, tie, 0.0))
    k_sel = min(N_SEL, nsel)
    return jnp.where(valid, jnp.where(rank < k_sel, 1.0, 0.0), 0.0)


def _expand_sel(sel_b, k0, tk, nbl):
    n_lane = lax.broadcasted_iota(jnp.int32, (nbl, tk), 0)
    key = k0 + lax.broadcasted_iota(jnp.int32, (nbl, tk), 1)
    e = jnp.where(n_lane == 2 * (key >> 6), 1.0, 0.0).astype(BF16)
    return _dot(sel_b, e)


def _nsa_prompt_kernel(q_ref, ck_ref, cv_ref, ks_ref, vs_ref, kw_ref, vw_ref, gate_ref, o_ref,
                       *, seq, nsel, tk):
    tq, rep, dh = NSA_TQ, NSA_REP, NSA_DH
    i = pl.program_id(2)
    l0 = i * tq
    qb = q_ref[...]
    q = jnp.concatenate([qb[:, r * dh:(r + 1) * dh] for r in range(rep)], axis=0)
    row = lax.broadcasted_iota(jnp.int32, (rep * tq, 1), 0)
    pos_rows = l0 + (row & (tq - 1))
    pos_l = l0 + lax.broadcasted_iota(jnp.int32, (tq, 1), 0)

    ck = ck_ref[...].astype(BF16)
    cv = cv_ref[...].astype(BF16)
    nbl = ck.shape[0]
    p, o_cmp = _cmp_branch(_dot_nt(q, ck), [cv], [(0, rep * tq)], pos_rows, nbl)
    imp = sum(p[r * tq:(r + 1) * tq] for r in range(rep))
    sel_b = _select_blocks(imp, pos_l, nsel).astype(BF16)

    def slc_step(t, carry):
        m_i, l_i, acc = carry
        k0 = pl.multiple_of(t * tk, tk)
        k = ks_ref[pl.ds(k0, tk), :].astype(BF16)
        v = vs_ref[pl.ds(k0, tk), :].astype(BF16)
        key = k0 + lax.broadcasted_iota(jnp.int32, (1, tk), 1)
        selexp = _expand_sel(sel_b, k0, tk, nbl)
        bias = jnp.where(jnp.where(key <= pos_l, selexp, 0.0) > 0.5, 0.0, NEG_INF)
        s = _dot_nt(q, k) + jnp.concatenate([bias] * rep, axis=0)
        m_new = jnp.maximum(m_i, jnp.max(s, axis=1, keepdims=True))
        alpha = jnp.exp(m_i - m_new)
        pe = jnp.exp(s - m_new)
        l_new = alpha * l_i + jnp.sum(pe, axis=1, keepdims=True)
        acc = alpha * acc + _dot(pe.astype(BF16), v)
        return m_new, l_new, acc

    n_kt = (l0 + tq + tk - 1) // tk
    init = (jnp.full((rep * tq, 1), NEG_INF, F32), jnp.zeros((rep * tq, 1), F32),
            jnp.zeros((rep * tq, dh), F32))
    _, l_f, acc = lax.fori_loop(0, n_kt, slc_step, init)
    o_slc = acc / l_f

    span = min(seq, WINDOW + tq)
    start = pl.multiple_of(jnp.maximum(l0 - WINDOW, 0), tq)
    kw = kw_ref[pl.ds(start, span), :].astype(BF16)
    vw = vw_ref[pl.ds(start, span), :].astype(BF16)
    key = start + lax.broadcasted_iota(jnp.int32, (1, span), 1)
    okw = jnp.where(key <= pos_l, jnp.where(key > pos_l - WINDOW, 1.0, 0.0), 0.0)
    bias = jnp.where(okw > 0.5, 0.0, NEG_INF)
    s = _dot_nt(q, kw) + jnp.concatenate([bias] * rep, axis=0)
    pw = jnp.exp(s - jnp.max(s, axis=1, keepdims=True))
    o_win = _dot(pw.astype(BF16), vw) / jnp.sum(pw, axis=1, keepdims=True)

    gt = jax.nn.sigmoid(gate_ref[...])
    for r in range(rep):
        rs = slice(r * tq, (r + 1) * tq)
        o_r = (gt[:, 3 * r:3 * r + 1] * o_cmp[rs] + gt[:, 3 * r + 1:3 * r + 2] * o_slc[rs]
               + gt[:, 3 * r + 2:3 * r + 3] * o_win[rs])
        o_ref[:, r * dh:(r + 1) * dh] = o_r.astype(o_ref.dtype)


def nsa_attn_prompt(q, ckv, new_slc, new_win, gates_g, *, bt, nsel):
    m = q.shape[0]
    L = m // bt
    tq = NSA_TQ
    assert L % tq == 0 and L >= WINDOW + tq
    nq = L // tq
    nbl = ckv.shape[3]
    tk = 512
    gw = NSA_REP * NSA_DH
    slc3 = new_slc.reshape(bt, L, NSA_ROW_W)
    win3 = new_win.reshape(bt, L, NSA_ROW_W)
    kv_spec = lambda off: pl.BlockSpec((None, L, NSA_DH), lambda b, g, i: (b, 0, off + g))
    return pl.pallas_call(
        functools.partial(_nsa_prompt_kernel, seq=L, nsel=nsel, tk=tk),
        grid=(bt, NSA_KVH, nq),
        in_specs=[
            pl.BlockSpec((tq, gw), lambda b, g, i: (b * nq + i, g)),
            pl.BlockSpec((None, None, None, nbl, NSA_DH), lambda b, g, i: (0, g, b, 0, 0)),
            pl.BlockSpec((None, None, None, nbl, NSA_DH), lambda b, g, i: (1, g, b, 0, 0)),
            kv_spec(0), kv_spec(NSA_KVH), kv_spec(0), kv_spec(NSA_KVH),
            pl.BlockSpec((None, tq, LANES), lambda b, g, i: (g, b * nq + i, 0)),
        ],
        out_specs=pl.BlockSpec((tq, gw), lambda b, g, i: (b * nq + i, g)),
        out_shape=jax.ShapeDtypeStruct((m, NSA_HEADS * NSA_DH), BF16),
        compiler_params=_cparams("parallel", "parallel", "arbitrary"),
        name="nsa_attn_prompt",
    )(q, ckv, ckv, slc3, slc3, win3, win3, gates_g)


def _nsa_sample_kernel(pt_ref, qg_ref, ck_ref, cv_ref, page_ref, new_ref, win_ref, gate_ref,
                       o_ref, sel_scr, m_scr, l_scr, acc_scr, ocmp_scr,
                       *, dseq, nsel, pos0, n_pages, n_new, n_win):
    del pt_ref
    G, rep = NSA_KVH, NSA_REP
    rows = G * dseq * rep
    rpg = dseq * rep
    p_id = pl.program_id(1)
    row = lax.broadcasted_iota(jnp.int32, (rows, 1), 0)
    pos_rows = pos0 + ((row >> 3) & (dseq - 1))
    nbl = sel_scr.shape[1]
    groups = [(g * rpg, (g + 1) * rpg) for g in range(G)]

    @pl.when(p_id == 0)
    def _():
        s = jnp.concatenate([_dot_nt(qg_ref[g], ck_ref[g].astype(BF16)) for g in range(G)], axis=0)
        p, o_cmp = _cmp_branch(s, [cv_ref[g].astype(BF16) for g in range(G)], groups, pos_rows, nbl)
        ocmp_scr[...] = o_cmp
        imp = jnp.sum(p.reshape(rows // rep, rep, nbl), axis=1, keepdims=True)
        imp = jnp.broadcast_to(imp, (rows // rep, rep, nbl)).reshape(rows, nbl)
        sel_scr[...] = _select_blocks(imp, pos_rows, nsel)
        m_scr[...] = jnp.full(m_scr.shape, NEG_INF, F32)
        l_scr[...] = jnp.zeros(l_scr.shape, F32)
        acc_scr[...] = jnp.zeros(acc_scr.shape, F32)

    def scores(src_ref):
        return jnp.concatenate(
            [_dot_nt(qg_ref[g], _page_part(src_ref, 0, g).astype(BF16)) for g in range(G)], axis=0)

    def weighted_values(pb, src_ref):
        return jnp.concatenate(
            [_dot(pb[r0:r1], _page_part(src_ref, 1, g).astype(BF16)) for g, (r0, r1) in enumerate(groups)],
            axis=0)

    def attend(src_ref, bias):
        s = scores(src_ref) + bias
        m_old = m_scr[...]
        m_new = jnp.maximum(m_old, jnp.max(s, axis=1, keepdims=True))
        alpha = jnp.exp(m_old - m_new)
        pe = jnp.where(bias == 0.0, jnp.exp(s - m_new), 0.0)
        l_scr[...] = alpha * l_scr[...] + jnp.sum(pe, axis=1, keepdims=True)
        acc_scr[...] = alpha * acc_scr[...] + weighted_values(pe.astype(BF16), src_ref)
        m_scr[...] = m_new

    k0 = p_id * PAGE_SIZE
    key = k0 + lax.broadcasted_iota(jnp.int32, (1, PAGE_SIZE), 1)
    selexp = _expand_sel(sel_scr[...].astype(BF16), k0, PAGE_SIZE, nbl)
    attend(page_ref, jnp.where(jnp.where(key <= pos_rows, selexp, 0.0) > 0.5, 0.0, NEG_INF))

    @pl.when(p_id == n_pages - 1)
    def _():
        past = n_pages * PAGE_SIZE
        nk = new_ref.shape[0]
        idx = lax.broadcasted_iota(jnp.int32, (1, nk), 1)
        selexp_n = _expand_sel(sel_scr[...].astype(BF16), past, nk, nbl)
        ok = jnp.where(idx < n_new, jnp.where(past + idx <= pos_rows, selexp_n, 0.0), 0.0)
        attend(new_ref, jnp.where(ok > 0.5, 0.0, NEG_INF))
        o_slc = acc_scr[...] / l_scr[...]
        nw = win_ref.shape[0]
        widx = lax.broadcasted_iota(jnp.int32, (1, nw), 1)
        kpos = pos0 - (n_win - n_new) + widx
        okw = jnp.where(widx < n_win, 1.0, 0.0)
        okw = jnp.where(kpos >= 0, okw, 0.0)
        okw = jnp.where(kpos <= pos_rows, okw, 0.0)
        okw = jnp.where(kpos > pos_rows - WINDOW, okw, 0.0)
        sw = scores(win_ref) + jnp.where(okw > 0.5, 0.0, NEG_INF)
        pw = jnp.exp(sw - jnp.max(sw, axis=1, keepdims=True))
        o_win = weighted_values(pw.astype(BF16), win_ref) / jnp.sum(pw, axis=1, keepdims=True)
        gt = jax.nn.sigmoid(gate_ref[...])
        o_ref[...] = gt[:, 0:1] * ocmp_scr[...] + gt[:, 1:2] * o_slc + gt[:, 2:3] * o_win


def nsa_attn_sample(page_table, qg, ckv, cache_slc, new_rows, win_rows, gates_s,
                    *, dseq, nsel, pos0, n_new, n_win):
    bt, n_pages = page_table.shape
    rows = NSA_KVH * dseq * NSA_REP
    nbl = ckv.shape[3]
    assert dseq & (dseq - 1) == 0 and NSA_REP == 8
    page_tail = cache_slc.shape[1:]
    return pl.pallas_call(
        functools.partial(_nsa_sample_kernel, dseq=dseq, nsel=nsel, pos0=pos0, n_pages=n_pages,
                          n_new=n_new, n_win=n_win),
        grid_spec=pltpu.PrefetchScalarGridSpec(
            num_scalar_prefetch=1,
            grid=(bt, n_pages),
            in_specs=[
                pl.BlockSpec((None, NSA_KVH, dseq * NSA_REP, NSA_DH), lambda b, p, pt: (b, 0, 0, 0)),
                pl.BlockSpec((None, NSA_KVH, None, nbl, NSA_DH), lambda b, p, pt: (0, 0, b, 0, 0)),
                pl.BlockSpec((None, NSA_KVH, None, nbl, NSA_DH), lambda b, p, pt: (1, 0, b, 0, 0)),
                pl.BlockSpec((None,) + page_tail, lambda b, p, pt: (pt[b, p],) + (0,) * len(page_tail)),
                pl.BlockSpec((None,) + new_rows.shape[1:], lambda b, p, pt: (b, 0, 0)),
                pl.BlockSpec((None,) + win_rows.shape[1:], lambda b, p, pt: (b, 0, 0)),
                pl.BlockSpec((None, rows, LANES), lambda b, p, pt: (b, 0, 0)),
            ],
            out_specs=pl.BlockSpec((None, rows, NSA_DH), lambda b, p, pt: (b, 0, 0)),
            scratch_shapes=[pltpu.VMEM((rows, nbl), F32), pltpu.VMEM((rows, 1), F32),
                            pltpu.VMEM((rows, 1), F32), pltpu.VMEM((rows, NSA_DH), F32),
                            pltpu.VMEM((rows, NSA_DH), F32)],
        ),
        out_shape=jax.ShapeDtypeStruct((bt, rows, NSA_DH), F32),
        compiler_params=_cparams("parallel", "arbitrary"),
        name="nsa_attn_sample",
    )(page_table, qg, ckv, ckv, cache_slc, new_rows, win_rows, gates_s)


def _round_up(x, mult):
    return -(-x // mult) * mult


def nsa_mixer(xb, w, *, bt, pos0, page_table=None, cache_cmp=None, cache_slc=None, cache_win=None):
    m = xb.shape[0]
    L = m // bt
    G, R, dh = NSA_KVH, NSA_REP, NSA_DH
    proj = mm(xb, w['in_main'])
    gates = mm(xb, w['in_gate'], bn=LANES)[:, :3 * NSA_HEADS]
    cos, sin = _rope_tables(pos0 + jnp.arange(L), dh)
    if L % 8:
        cos, sin = jnp.tile(cos, (bt, 1)), jnp.tile(sin, (bt, 1))
    q, new_cmp, new_slc, new_win = nsa_prep(proj, cos, sin, rows_per_table=cos.shape[0])
    past = 0 if page_table is None else page_table.shape[1] * PAGE_SIZE
    total = past + L
    tp = _round_up(total, SEL_BLOCK)
    nsel = tp // SEL_BLOCK
    nbl = _round_up(tp // CMP_BLOCK, LANES)

    if page_table is None:
        src = new_cmp.reshape(m // PAGE_SIZE, PAGE_SIZE, NSA_ROW_W)
        pt = jnp.arange(m // PAGE_SIZE, dtype=jnp.int32).reshape(bt, L // PAGE_SIZE)
    else:
        src = cache_cmp
        pt = page_table
    xc = cmp_relayout(src, pt)
    nb = xc.shape[2] // bt
    ccos, csin = _rope_tables((jnp.arange(nb) + 1) * CMP_BLOCK - 1, dh)
    ccos = jnp.stack([jnp.tile(ccos, (bt, 1)), jnp.ones((bt * nb, dh), F32)])
    csin = jnp.stack([jnp.tile(csin, (bt, 1)), jnp.zeros((bt * nb, dh), F32)])
    ckv = compress(xc, w['pe'], w['c1'], w['c2'], ccos, csin).reshape(2, G, bt, nb, dh)
    ckv = jnp.pad(ckv, ((0, 0), (0, 0), (0, 0), (0, nbl - nb), (0, 0)))

    g5 = gates.reshape(bt, L, G, R, 3)
    if page_table is None:
        gates_g = jnp.transpose(g5, (2, 0, 1, 3, 4)).reshape(G, m, R * 3)
        gates_g = jnp.pad(gates_g, ((0, 0), (0, 0), (0, LANES - R * 3)))
        o = nsa_attn_prompt(q, ckv, new_slc, new_win, gates_g, bt=bt, nsel=nsel)
        win_keep = new_win.reshape(bt, L, NSA_ROW_W)[:, L - min(WINDOW, L):]
    else:
        q5 = q.reshape(bt, L, G, R, dh)
        qg = jnp.transpose(q5, (0, 2, 1, 3, 4)).reshape(bt, G, L * R, dh)
        gates_s = jnp.transpose(g5, (0, 2, 1, 3, 4)).reshape(bt, G * L * R, 3)
        gates_s = jnp.pad(gates_s, ((0, 0), (0, 0), (0, LANES - 3)))
        new3 = new_slc.reshape(bt, L, NSA_ROW_W)
        new_pad = jnp.pad(new3, ((0, 0), (0, _round_up(L, LANES) - L), (0, 0)))
        prefix = cache_win.reshape(bt, -1, NSA_ROW_W)
        win_all = jnp.concatenate([prefix, new_win.reshape(bt, L, NSA_ROW_W)], axis=1)
        n_win = win_all.shape[1]
        win_pad = jnp.pad(win_all, ((0, 0), (0, _round_up(n_win, LANES) - n_win), (0, 0)))
        o = nsa_attn_sample(pt, qg, ckv, cache_slc, new_pad, win_pad,
                            gates_s, dseq=L, nsel=nsel, pos0=pos0, n_new=L, n_win=n_win)
        o = jnp.transpose(o.reshape(bt, G, L, R, dh), (0, 2, 1, 3, 4)).reshape(m, NSA_HEADS * dh).astype(BF16)
        win_keep = win_all[:, n_win - prefix.shape[1]:]
    shape5 = (bt, L, 2, G, dh)
    return o, new_cmp.reshape(shape5), new_slc.reshape(shape5), win_keep.reshape(bt, -1, 2, G, dh)


def kernel(x_prompt, x_sample, state_ssm, state_ret, cache_cmp_kv, cache_slc_kv, cache_win_kv, state_pool, page_table, ln_g, ln_b, ffn_w_gate, ffn_w_up, ffn_w_down, ssm_a_re, ssm_a_im, ssm_log_dt, ssm_b_re, ssm_b_im, ssm_c_re, ssm_c_im, ssm_d, ssm_w_glu, ret_w_in, ret_norm_g, ret_w_out, nsa_w_in, nsa_pe_ck, nsa_w_ck1, nsa_w_ck2, nsa_pe_cv, nsa_w_cv1, nsa_w_cv2, nsa_w_out, pool_w, pool_scale):
    past_len = page_table.shape[1] * PAGE_SIZE

    wg = ffn_w_gate
    wu = ffn_w_up
    wd = ffn_w_down.astype(BF16)
    w_glu = ssm_w_glu[0]
    r_w_in = ret_w_in[0]
    r_w_out = ret_w_out[0].astype(BF16)
    n_main = NSA_HEADS * NSA_DH + 6 * NSA_KV_W
    nsa_w = {
        'in_main': nsa_w_in[0, :, :n_main].astype(BF16),
        'in_gate': jnp.pad(nsa_w_in[0, :, n_main:], ((0, 0), (0, LANES - 3 * NSA_HEADS))).astype(BF16),
        'pe': jnp.stack([nsa_pe_ck[0], nsa_pe_cv[0]]).astype(F32),
        'c1': jnp.stack([nsa_w_ck1[0], nsa_w_cv1[0]]).astype(BF16),
        'c2': jnp.stack([nsa_w_ck2[0], nsa_w_cv2[0]]).astype(BF16),
    }
    n_w_out = nsa_w_out[0].astype(BF16)
    p_w = pool_w[0].astype(BF16)
    a_tab, bblk, cblk = _ssm_tables(ssm_a_re[0], ssm_a_im[0], ssm_log_dt[0], ssm_b_re[0], ssm_b_im[0],
                                    ssm_c_re[0], ssm_c_im[0])

    def ffn(xf, xb, i, s, ln_idx):
        h = dual_mm(xb, wg, wu, combine=_swiglu_combine, n=D_FF, widx_a=(i, s), widx_b=(i, s), bn=256)
        return mm_res_ln(h, wd, xf, ln_g[i, ln_idx], ln_b[i, ln_idx], widx=(i, s), scale=0.5)

    def stream(x, pos0, is_prompt):
        bt, L, _ = x.shape
        m = bt * L
        xf = x.astype(F32).reshape(m, D_MODEL)
        xb = xf.astype(BF16)
        outs = {}
        for i in range(DEPTH):
            mixer = i % N_MIXERS
            xf, xb = ffn(xf, xb, i, 0, 0)
            if mixer == 0:
                h0 = jnp.zeros((bt, SSM_GROUPS, SSM_N, 2), F32) if is_prompt else state_ssm[0]
                yg, outs['ssm'] = ssm_scan(xf, h0, a_tab, bblk, cblk, ssm_d[0], bt=bt)
                y = dual_mm(yg, w_glu, w_glu, combine=_glu_combine, n=D_MODEL, col0_b=D_MODEL,
                            out_dtype=BF16, bn=256)
                xf, xb = res_ln(y, xf, ln_g[i, 1], ln_b[i, 1])
            else:
                if mixer == 1:
                    proj = mm(xb, r_w_in, out_dtype=BF16)
                    s0 = None if is_prompt else state_ret[0]
                    o, outs['ret'] = retention_core(proj, s0, ret_norm_g[0], bt=bt, pos0=pos0)
                    w_o = r_w_out
                elif mixer == 2:
                    if is_prompt:
                        o, c_new, s_new, w_new = nsa_mixer(xb, nsa_w, bt=bt, pos0=pos0)
                    else:
                        o, c_new, s_new, w_new = nsa_mixer(
                            xb, nsa_w, bt=bt, pos0=pos0, page_table=page_table,
                            cache_cmp=cache_cmp_kv[0], cache_slc=cache_slc_kv[0], cache_win=cache_win_kv[0])
                    outs['cmp'], outs['slc'], outs['win'] = c_new, s_new, w_new
                    w_o = n_w_out
                else:
                    prefix = jnp.zeros((bt, POOL_PAST, D_MODEL), F32) if is_prompt else state_pool[0]
                    y = pool_core(xf, prefix, p_w, pool_scale[0], bt=bt, pos0=pos0)
                    x3 = xf.reshape(bt, L, D_MODEL)
                    if L < POOL_PAST:
                        x3 = jnp.concatenate([prefix.astype(F32)[:, L:], x3], axis=1)
                    outs['pool'] = x3[:, -POOL_PAST:]
                if mixer == 3:
                    xf, xb = res_ln(y, xf, ln_g[i, 1], ln_b[i, 1])
                else:
                    xf, xb = mm_res_ln(o, w_o, xf, ln_g[i, 1], ln_b[i, 1])
            xf, xb = ffn(xf, xb, i, 1, 2)
        return xf.reshape(bt, L, D_MODEL), outs

    yp, op = stream(x_prompt, 0, True)
    ys, os_ = stream(x_sample, past_len, False)
    return (yp.astype(x_prompt.dtype), ys.astype(x_sample.dtype),
            op['ssm'][None], os_['ssm'][None], op['ret'][None], os_['ret'][None],
            op['cmp'][None], os_['cmp'][None], op['slc'][None], os_['slc'][None],
            op['win'][None], os_['win'][None], op['pool'][None], os_['pool'][None])
```
